```python
import math
import jax, jax.numpy as jnp
from jax import lax
import numpy as np

D_MODEL = 1024
BATCH = 8
SEQ = 4096
DEPTH = 4
DEC_BATCH = 8
DEC_SEQ = 16
PAST_LEN = 2048

CHUNK = 64
EPS = 1e-6
N_AB = (DEPTH + 1) // 2
N_C = DEPTH // 2
DN_HEADS = 8
DN_DK = 64
DN_DV = 64
CONV_W = 4
DN_QKV = DN_HEADS * (2 * DN_DK + DN_DV)
GLA_HEADS = 8
GLA_DK = 32
GLA_DV = 64
GLA_RANK = 16
GLA_NORMALIZER = 16.0
AB_SIZES = (DN_HEADS * DN_DK, DN_HEADS * DN_DK, DN_HEADS * DN_DV,
            DN_HEADS, DN_HEADS, DN_HEADS * DN_DV,
            GLA_HEADS * GLA_DK, GLA_HEADS * GLA_DK, GLA_HEADS * GLA_DV,
            GLA_RANK, GLA_HEADS * GLA_DV)
AB_IN = sum(AB_SIZES)
AB_MIX = DN_HEADS * DN_DV + GLA_HEADS * GLA_DV
C_HEADS = 16
C_HD = 64
C_BAND_CHUNKS = 8
C_PAST = C_BAND_CHUNKS * CHUNK
C_BAND = C_PAST + CHUNK
MAX_REL = 128
D_FF = ((8 * D_MODEL // 3 + 255) // 256) * 256

kernel_name = 'hybrid_streaming_gdn_gla_chunkattn_step'


def _rmsnorm(x, g):
    xf = x.astype(jnp.float32)
    y = xf * lax.rsqrt(jnp.mean(xf * xf, axis=-1, keepdims=True) + EPS)
    return (y * g.astype(jnp.float32)).astype(x.dtype)


def _l2norm(x):
    return x * lax.rsqrt(jnp.sum(x * x, axis=-1, keepdims=True) + EPS)


def _split(x, sizes):
    return jnp.split(x, [int(s) for s in np.cumsum(sizes)[:-1]], axis=-1)


def _causal_conv(x, buf, w):
    L = x.shape[1]
    xp = jnp.concatenate([buf.astype(x.dtype), x], axis=1)
    y = xp[:, 0:L] * w[0]
    for i in range(1, CONV_W):
        y = y + xp[:, i:i + L] * w[i]
    return jax.nn.silu(y), xp[:, L:]


def _chunks(x, c):
    b, l, h = x.shape[:3]
    x = x.reshape((b, l // c, c, h) + tuple(x.shape[3:]))
    return jnp.moveaxis(x, (1, 3), (0, 2))


def _unchunks(o):
    n, b, h, c, d = o.shape
    return jnp.moveaxis(o, (0, 2), (1, 3)).reshape(b, n * c, h, d)


def _gated_delta(q, k, v, g, beta, s0):
    L = q.shape[1]
    c = min(CHUNK, L)
    causal = jnp.tril(jnp.ones((c, c), dtype=bool))
    strict = jnp.tril(jnp.ones((c, c), dtype=bool), -1)
    eye = jnp.eye(c, dtype=jnp.float32)

    def step(s, inp):
        qc, kc, vc, gc, bc = inp
        gcum = jnp.cumsum(gc, axis=-1)
        decay = jnp.exp(jnp.where(causal, gcum[..., :, None] - gcum[..., None, :], -jnp.inf))
        kb = kc * bc[..., None]
        a = jnp.where(strict, jnp.einsum('bhid,bhjd->bhij', kb, kc) * decay, 0.0)
        rhs = jnp.concatenate([vc * bc[..., None], kb * jnp.exp(gcum)[..., None]], axis=-1)
        sol = lax.linalg.triangular_solve(eye + a, rhs, left_side=True, lower=True)
        dv = vc.shape[-1]
        u, wk = sol[..., :dv], sol[..., dv:]
        v_new = u - jnp.einsum('bhcd,bhde->bhce', wk, s)
        o = (jnp.einsum('bhcd,bhde->bhce', qc * jnp.exp(gcum)[..., None], s)
             + jnp.einsum('bhij,bhje->bhie', jnp.einsum('bhid,bhjd->bhij', qc, kc) * decay, v_new))
        glast = gcum[..., -1:]
        s = (s * jnp.exp(glast)[..., None]
             + jnp.einsum('bhcd,bhce->bhde', kc * jnp.exp(glast - gcum)[..., None], v_new))
        return s, o

    xs = (_chunks(q, c), _chunks(k, c), _chunks(v, c), _chunks(g, c), _chunks(beta, c))
    s, o = lax.scan(step, s0.astype(jnp.float32), xs)
    return _unchunks(o), s


def _gla(q, k, v, gk, s0):
    L = q.shape[1]
    c = min(CHUNK, L)
    causal = jnp.tril(jnp.ones((c, c), dtype=bool))

    def step(s, inp):
        qc, kc, vc, gc = inp
        b = jnp.cumsum(gc, axis=2)
        o_inter = jnp.einsum('bhcd,bhde->bhce', qc * jnp.exp(b), s)
        diff = b[:, :, :, None, :] - b[:, :, None, :, :]
        decay = jnp.exp(jnp.where(causal[:, :, None], diff, -jnp.inf))
        a = jnp.einsum('bhid,bhjd,bhijd->bhij', qc, kc, decay)
        o = o_inter + jnp.einsum('bhij,bhje->bhie', a, vc)
        blast = b[:, :, -1:, :]
        s = (s * jnp.exp(blast[:, :, 0, :])[..., None]
             + jnp.einsum('bhcd,bhce->bhde', kc * jnp.exp(blast - b), vc))
        return s, o

    xs = (_chunks(q, c), _chunks(k, c), _chunks(v, c), _chunks(gk, c))
    s, o = lax.scan(step, s0.astype(jnp.float32), xs)
    return _unchunks(o), s


def _ab_mixer(h, conv_buf, dn_s0, gla_s0, w_in, conv_w, a_log, dt_bias, dn_onorm,
              gk_w2, gk_b, gla_onorm, w_out):
    B, L, _ = h.shape
    f32 = jnp.float32
    proj = h @ w_in
    qkv_raw, dn_a, dn_b, dn_gate, gla_q, gla_k, gla_v, gla_lr, gla_gate = _split(
        proj, (DN_QKV,) + AB_SIZES[3:])
    qkv, conv_new = _causal_conv(qkv_raw, conv_buf, conv_w)
    dq, dk, dv = _split(qkv, (DN_HEADS * DN_DK, DN_HEADS * DN_DK, DN_HEADS * DN_DV))
    q = _l2norm(dq.reshape(B, L, DN_HEADS, DN_DK).astype(f32)) * (DN_DK ** -0.5)
    k = _l2norm(dk.reshape(B, L, DN_HEADS, DN_DK).astype(f32))
    v = dv.reshape(B, L, DN_HEADS, DN_DV).astype(f32)
    beta = jax.nn.sigmoid(dn_b.astype(f32))
    g = -jnp.exp(a_log.astype(f32)) * jax.nn.softplus(dn_a.astype(f32) + dt_bias.astype(f32))
    o_dn, dn_s = _gated_delta(q, k, v, g, beta, dn_s0)
    o_dn = _rmsnorm(o_dn, dn_onorm) * jax.nn.silu(dn_gate.reshape(B, L, DN_HEADS, DN_DV).astype(f32))
    q2 = gla_q.reshape(B, L, GLA_HEADS, GLA_DK).astype(f32) * (GLA_DK ** -0.5)
    k2 = gla_k.reshape(B, L, GLA_HEADS, GLA_DK).astype(f32)
    v2 = gla_v.reshape(B, L, GLA_HEADS, GLA_DV).astype(f32)
    glog = jax.nn.log_sigmoid((gla_lr @ gk_w2 + gk_b).astype(f32)) / GLA_NORMALIZER
    glog = glog.reshape(B, L, GLA_HEADS, GLA_DK)
    o_gla, gla_s = _gla(q2, k2, v2, glog, gla_s0)
    o_gla = _rmsnorm(o_gla, gla_onorm) * jax.nn.silu(gla_gate.reshape(B, L, GLA_HEADS, GLA_DV).astype(f32))
    o = jnp.concatenate([o_dn.reshape(B, L, -1), o_gla.reshape(B, L, -1)], axis=-1).astype(h.dtype)
    return o @ w_out, conv_new, dn_s, gla_s


def _rel_bias(table, rel):
    return table[:, jnp.clip(rel, -MAX_REL, MAX_REL) + MAX_REL].astype(jnp.float32)


def _band_attn_prompt(q, k, v, table):
    B, L, H, hd = q.shape
    n = L // CHUNK
    zpad = jnp.zeros((B, C_PAST, H, hd), k.dtype)
    kp = jnp.concatenate([zpad, k], axis=1)
    vp = jnp.concatenate([zpad, v], axis=1)
    qi = jnp.arange(CHUNK)
    kj = jnp.arange(C_BAND)
    bias = _rel_bias(table, qi[:, None] + C_PAST - kj[None, :])
    qc = jnp.moveaxis(q.reshape(B, n, CHUNK, H, hd), 1, 0)

    def one(args):
        c, qb = args
        kb = lax.dynamic_slice_in_dim(kp, c * CHUNK, C_BAND, axis=1)
        vb = lax.dynamic_slice_in_dim(vp, c * CHUNK, C_BAND, axis=1)
        valid = (c * CHUNK - C_PAST + kj) >= 0
        s = jnp.einsum('bqhd,bkhd->bhqk', qb, kb).astype(jnp.float32) * (hd ** -0.5) + bias
        p = jax.nn.softmax(jnp.where(valid, s, -jnp.inf), axis=-1).astype(vb.dtype)
        return jnp.einsum('bhqk,bkhd->bqhd', p, vb)

    o = lax.map(one, (jnp.arange(n), qc))
    return jnp.moveaxis(o, 0, 1).reshape(B, L, H * hd)


def _band_attn_sample(q, k, v, k_cache, v_cache, table):
    B, L, H, hd = q.shape
    nc = k_cache.shape[1]
    kk = jnp.concatenate([k_cache.astype(k.dtype), k], axis=1)
    vv = jnp.concatenate([v_cache.astype(v.dtype), v], axis=1)
    qpos = PAST_LEN + jnp.arange(L)
    kpos = jnp.concatenate([PAST_LEN - nc + jnp.arange(nc), PAST_LEN + jnp.arange(L)])
    bias = _rel_bias(table, qpos[:, None] - kpos[None, :])
    qch = qpos // CHUNK
    kch = kpos // CHUNK
    valid = (kch[None, :] <= qch[:, None]) & (kch[None, :] >= qch[:, None] - C_BAND_CHUNKS)
    s = jnp.einsum('bqhd,bkhd->bhqk', q, kk).astype(jnp.float32) * (hd ** -0.5) + bias
    p = jax.nn.softmax(jnp.where(valid, s, -jnp.inf), axis=-1).astype(vv.dtype)
    return jnp.einsum('bhqk,bkhd->bqhd', p, vv).reshape(B, L, H * hd)


def _swiglu(h, w_gu, w_down):
    gate, up = jnp.split(h @ w_gu, 2, axis=-1)
    return (jax.nn.silu(gate) * up) @ w_down


def _trunk(x, conv0, dn0, gla0, ck, cv, w, prompt):
    B, L, _ = x.shape
    convs, dns, glas, ks, vs = [], [], [], [], []
    for layer in range(DEPTH):
        i = layer // 2
        if layer % 2 == 0:
            h = _rmsnorm(x, w['ab_norm'][i])
            y, cb, sd, sg = _ab_mixer(h, conv0[i], dn0[i], gla0[i], w['ab_w_in'][i], w['dn_conv_w'][i],
                                      w['dn_a_log'][i], w['dn_dt_bias'][i], w['dn_out_norm'][i],
                                      w['gla_gk_w2'][i], w['gla_gk_b'][i], w['gla_out_norm'][i],
                                      w['ab_w_out'][i])
            convs.append(cb)
            dns.append(sd)
            glas.append(sg)
        else:
            h = _rmsnorm(x, w['c_norm'][i])
            q, k, v = jnp.split(h @ w['c_w_qkv'][i], 3, axis=-1)
            q = q.reshape(B, L, C_HEADS, C_HD)
            k = k.reshape(B, L, C_HEADS, C_HD)
            v = v.reshape(B, L, C_HEADS, C_HD)
            if prompt:
                o = _band_attn_prompt(q, k, v, w['c_rel_bias'][i])
                rows = min(C_PAST, L)
                ks.append(k[:, L - rows:])
                vs.append(v[:, L - rows:])
            else:
                o = _band_attn_sample(q, k, v, ck[i], cv[i], w['c_rel_bias'][i])
                ks.append(k)
                vs.append(v)
            y = o.astype(x.dtype) @ w['c_w_out'][i]
        x = x + y.astype(x.dtype)
        x = x + _swiglu(_rmsnorm(x, w['ffn_norm'][layer]), w['ffn_w_gu'][layer],
                        w['ffn_w_down'][layer]).astype(x.dtype)
    y_out = _rmsnorm(x, w['final_norm'])
    dt = x.dtype
    return (y_out, jnp.stack(convs).astype(dt), jnp.stack(dns).astype(dt), jnp.stack(glas).astype(dt),
            jnp.stack(ks).astype(dt), jnp.stack(vs).astype(dt))


def setup_inputs(seed: int = 0) -> dict:
    key = jax.random.key(seed)
    ks = jax.random.split(key, 32)
    f32 = jnp.float32

    def nrm(k, shape, scale):
        return scale * jax.random.normal(k, shape, f32)

    c_rows = min(C_PAST, PAST_LEN)
    dt = jnp.exp(jax.random.uniform(ks[10], (N_AB, DN_HEADS), f32, math.log(1e-3), math.log(1e-1)))
    return {
        'x_prompt': nrm(ks[0], (BATCH, SEQ, D_MODEL), 1.0),
        'x_sample': nrm(ks[1], (DEC_BATCH, DEC_SEQ, D_MODEL), 1.0),
        'state_dn_conv': nrm(ks[2], (N_AB, DEC_BATCH, CONV_W - 1, DN_QKV), 1.0),
        'state_dn': nrm(ks[3], (N_AB, DEC_BATCH, DN_HEADS, DN_DK, DN_DV), 0.5),
        'state_gla': nrm(ks[4], (N_AB, DEC_BATCH, GLA_HEADS, GLA_DK, GLA_DV), 1.0),
        'cache_c_k': nrm(ks[5], (N_C, DEC_BATCH, c_rows, C_HEADS, C_HD), 1.0),
        'cache_c_v': nrm(ks[6], (N_C, DEC_BATCH, c_rows, C_HEADS, C_HD), 1.0),
        'ab_norm': 1.0 + nrm(ks[7], (N_AB, D_MODEL), 0.01),
        'ab_w_in': nrm(ks[8], (N_AB, D_MODEL, AB_IN), D_MODEL ** -0.5),
        'dn_conv_w': nrm(ks[9], (N_AB, CONV_W, DN_QKV), CONV_W ** -0.5),
        'dn_a_log': jnp.log(jax.random.uniform(ks[11], (N_AB, DN_HEADS), f32, 1.0, 16.0)),
        'dn_dt_bias': dt + jnp.log(-jnp.expm1(-dt)),
        'dn_out_norm': 1.0 + nrm(ks[12], (N_AB, DN_DV), 0.01),
        'gla_gk_w2': nrm(ks[13], (N_AB, GLA_RANK, GLA_HEADS * GLA_DK), GLA_RANK ** -0.5),
        'gla_gk_b': nrm(ks[14], (N_AB, GLA_HEADS * GLA_DK), 0.1),
        'gla_out_norm': 1.0 + nrm(ks[15], (N_AB, GLA_DV), 0.01),
        'ab_w_out': nrm(ks[16], (N_AB, AB_MIX, D_MODEL), AB_MIX ** -0.5),
        'c_norm': 1.0 + nrm(ks[17], (N_C, D_MODEL), 0.01),
        'c_w_qkv': nrm(ks[18], (N_C, D_MODEL, 3 * C_HEADS * C_HD), D_MODEL ** -0.5),
        'c_rel_bias': nrm(ks[19], (N_C, C_HEADS, 2 * MAX_REL + 1), 0.5),
        'c_w_out': nrm(ks[20], (N_C, C_HEADS * C_HD, D_MODEL), (C_HEADS * C_HD) ** -0.5),
        'ffn_norm': 1.0 + nrm(ks[21], (DEPTH, D_MODEL), 0.01),
        'ffn_w_gu': nrm(ks[22], (DEPTH, D_MODEL, 2 * D_FF), D_MODEL ** -0.5),
        'ffn_w_down': nrm(ks[23], (DEPTH, D_FF, D_MODEL), D_FF ** -0.5),
        'final_norm': 1.0 + nrm(ks[24], (D_MODEL,), 0.01),
    }


def reference(x_prompt, x_sample, state_dn_conv, state_dn, state_gla, cache_c_k, cache_c_v,
              ab_norm, ab_w_in, dn_conv_w, dn_a_log, dn_dt_bias, dn_out_norm, gla_gk_w2, gla_gk_b,
              gla_out_norm, ab_w_out, c_norm, c_w_qkv, c_rel_bias, c_w_out, ffn_norm, ffn_w_gu,
              ffn_w_down, final_norm):
    w = {'ab_norm': ab_norm, 'ab_w_in': ab_w_in, 'dn_conv_w': dn_conv_w, 'dn_a_log': dn_a_log,
         'dn_dt_bias': dn_dt_bias, 'dn_out_norm': dn_out_norm, 'gla_gk_w2': gla_gk_w2,
         'gla_gk_b': gla_gk_b, 'gla_out_norm': gla_out_norm, 'ab_w_out': ab_w_out, 'c_norm': c_norm,
         'c_w_qkv': c_w_qkv, 'c_rel_bias': c_rel_bias, 'c_w_out': c_w_out, 'ffn_norm': ffn_norm,
         'ffn_w_gu': ffn_w_gu, 'ffn_w_down': ffn_w_down, 'final_norm': final_norm}
    B = x_prompt.shape[0]
    conv0 = jnp.zeros((N_AB, B, CONV_W - 1, DN_QKV), x_prompt.dtype)
    dn0 = jnp.zeros((N_AB, B, DN_HEADS, DN_DK, DN_DV), jnp.float32)
    gla0 = jnp.zeros((N_AB, B, GLA_HEADS, GLA_DK, GLA_DV), jnp.float32)
    y_prompt, conv_p, dn_p, gla_p, ck_p, cv_p = _trunk(x_prompt, conv0, dn0, gla0, None, None, w, True)
    y_sample, conv_s, dn_s, gla_s, ck_s, cv_s = _trunk(x_sample, state_dn_conv, state_dn, state_gla,
                                                       cache_c_k, cache_c_v, w, False)
    return (y_prompt, y_sample, conv_p, conv_s, dn_p, dn_s, gla_p, gla_s, ck_p, ck_s, cv_p, cv_s)
```

```python
import functools
import math

import numpy as np
import jax
import jax.numpy as jnp
from jax import lax
from jax.experimental import pallas as pl
from jax.experimental.pallas import tpu as pltpu

F32 = jnp.float32
BF16 = jnp.bfloat16

EPS = 1e-6
CHUNK = 64
PAST_LEN = 2048
D_MODEL = 1024
DN_HEADS = 8
DN_DK = 64
DN_DV = 64
CONV_W = 4
DN_QKV = DN_HEADS * (2 * DN_DK + DN_DV)
GLA_HEADS = 8
GLA_DK = 32
GLA_DV = 64
GLA_RANK = 16
GLA_NORMALIZER = 16.0
GLA_SUB = 16
C_HEADS = 16
C_HD = 64
C_BAND_CHUNKS = 8
C_PAST = C_BAND_CHUNKS * CHUNK
MAX_REL = 128
ATT_QB = 256

H = DN_HEADS
HW = DN_HEADS * DN_DV
GW = GLA_HEADS * GLA_DK
LANE = 128
SUBLANE = 8

OFF_QKV = 0
OFF_DGATE = DN_QKV
OFF_GQ = OFF_DGATE + HW
OFF_GK = OFF_GQ + GW
OFF_GV = OFF_GK + GW
OFF_GGATE = OFF_GV + HW
OFF_SMALL = OFF_GGATE + HW
AB_PAD = OFF_SMALL + LANE

VMEM_LIMIT = 56 * 1024 * 1024


def _mm(a, b):
    return jnp.dot(a, b, preferred_element_type=F32)


def _mm_nt(a, b):
    return lax.dot_general(a, b, (((1,), (1,)), ((), ())), preferred_element_type=F32)


def _mm_tn(a, b):
    return lax.dot_general(a, b, (((0,), (0,)), ((), ())), preferred_element_type=F32)


def _split(x, n):
    parts = []
    r = x
    for t in range(n):
        p = r.astype(BF16)
        parts.append(p)
        if t + 1 < n:
            r = r - p.astype(F32)
    return parts


def _mmx(x, m, n):
    out = None
    for p in _split(x, n):
        t = _mm(p, m)
        out = t if out is None else out + t
    return out


def _mmx_left(m, x, n):
    out = None
    for p in _split(x, n):
        t = _mm(m, p)
        out = t if out is None else out + t
    return out


def _softplus(x):
    return jnp.maximum(x, 0.0) + jnp.log(1.0 + jnp.exp(-jnp.abs(x)))


def _tile_rows(x, n):
    return jnp.concatenate([x] * n, axis=0)


def _rms(x, g):
    ms = jnp.mean(x * x, axis=-1, keepdims=True)
    return (x * lax.rsqrt(ms + EPS)) * g


def _const_spec(shape):
    nd = len(shape)
    return pl.BlockSpec(shape, lambda *_: (0,) * nd, pipeline_mode=pl.Buffered(1))


def _pre_kernel(x_ref, g_ref, w_ref, o_ref, *, col_chunk):
    xn = _rms(x_ref[...], g_ref[...]).astype(BF16)
    n = o_ref.shape[-1]
    for c0 in range(0, n, col_chunk):
        c1 = min(n, c0 + col_chunk)
        o_ref[:, c0:c1] = _mm(xn, w_ref[:, c0:c1])


def _pre(x, g, w, tm):
    t, d = x.shape
    n = w.shape[1]
    return pl.pallas_call(
        functools.partial(_pre_kernel, col_chunk=512),
        grid=(t // tm,),
        in_specs=[pl.BlockSpec((tm, d), lambda i: (i, 0)),
                  _const_spec((1, d)),
                  _const_spec((d, n))],
        out_specs=pl.BlockSpec((tm, n), lambda i: (i, 0)),
        out_shape=jax.ShapeDtypeStruct((t, n), F32),
        compiler_params=pltpu.CompilerParams(dimension_semantics=("parallel",),
                                             vmem_limit_bytes=VMEM_LIMIT),
        name="pre_proj",
    )(x, g.reshape(1, d), w)


def _post_kernel(o_ref, x_ref, wo_ref, g_ref, wgu_ref, wd_ref, gf_ref, out_ref, acc_ref, *, ff_chunk, final):
    x1 = x_ref[...] + _mm(o_ref[...], wo_ref[...])
    xn = _rms(x1, g_ref[...]).astype(BF16)
    dff = wd_ref.shape[0]
    for f0 in range(0, dff, ff_chunk):
        gate = _mm(xn, wgu_ref[:, f0:f0 + ff_chunk])
        up = _mm(xn, wgu_ref[:, dff + f0:dff + f0 + ff_chunk])
        a = (gate * jax.nn.sigmoid(gate) * up).astype(BF16)
        part = _mm(a, wd_ref[f0:f0 + ff_chunk, :])
        if f0 == 0:
            acc_ref[...] = part
        else:
            acc_ref[...] += part
    x2 = x1 + acc_ref[...]
    if final:
        x2 = _rms(x2, gf_ref[...])
    out_ref[...] = x2


def _post(o, x, wo, g, wgu, wd, gf, tm, final):
    t, d = x.shape
    dm = o.shape[1]
    dff = wd.shape[0]
    return pl.pallas_call(
        functools.partial(_post_kernel, ff_chunk=256, final=final),
        grid=(t // tm,),
        in_specs=[pl.BlockSpec((tm, dm), lambda i: (i, 0)),
                  pl.BlockSpec((tm, d), lambda i: (i, 0)),
                  _const_spec((dm, d)),
                  _const_spec((1, d)),
                  _const_spec((d, 2 * dff)),
                  _const_spec((dff, d)),
                  _const_spec((1, d))],
        out_specs=pl.BlockSpec((tm, d), lambda i: (i, 0)),
        out_shape=jax.ShapeDtypeStruct((t, d), F32),
        scratch_shapes=[pltpu.VMEM((tm, d), F32)],
        compiler_params=pltpu.CompilerParams(dimension_semantics=("parallel",),
                                             vmem_limit_bytes=VMEM_LIMIT),
        name="post_ffn",
    )(o, x, wo, g.reshape(1, d), wgu, wd, gf.reshape(1, d))


@functools.lru_cache(maxsize=None)
def _ab_consts(c):
    w = H * c
    nsb = c // GLA_SUB
    i = np.arange(c)
    hw_head = np.arange(HW) // DN_DV
    gw_head = np.arange(GW) // GLA_DK
    w_head = np.arange(w) // c
    w_pos = np.arange(w) % c
    ltri = (i[None, :] <= i[:, None])
    eye_w = (i[:, None] == w_pos[None, :])
    causal_w = (w_pos[None, :] <= i[:, None])
    strict_w = (w_pos[None, :] < i[:, None])
    dmask = causal_w & ((i[:, None] // GLA_SUB) == (w_pos[None, :] // GLA_SUB))
    m_cc = (w_head[:, None] == w_head[None, :])
    m_cd = (w_head[:, None] == hw_head[None, :])
    m_gk = (w_head[:, None] == gw_head[None, :])
    m_gk3 = np.concatenate([m_gk & ((w_pos // GLA_SUB) == j)[:, None] for j in range(max(nsb - 1, 1))], axis=1)
    m_s = (hw_head[:, None] == hw_head[None, :])
    m_st = (hw_head[:, None] == gw_head[None, :])
    mg = (w_head[:, None] == (np.arange(LANE) // 16)[None, :])
    sel = np.zeros((6, LANE, LANE), np.float32)
    cst = np.zeros((SUBLANE, LANE), np.float32)
    for h in range(H):
        for p in range(3):
            sel[p, h, 16 * h + p] = 1.0
            sel[3 + p, h, 16 * h + 3 + p] = -1.0
            cst[0, 16 * h + 3 + p] = 1.0
            cst[1, 16 * h + p] = 1.0
    expg = np.zeros((LANE, HW), np.float32)
    expb = np.zeros((LANE, HW), np.float32)
    for h in range(H):
        expg[h, DN_DV * h:DN_DV * (h + 1)] = 1.0
        expb[H + h, DN_DV * h:DN_DV * (h + 1)] = 1.0
    f = lambda a: jnp.asarray(a, F32)
    b = lambda a: jnp.asarray(a, BF16)
    return dict(ltri=b(ltri), eye_w=f(eye_w), causal_w=f(causal_w), strict_w=f(strict_w), dmask=f(dmask),
                m_cc=b(m_cc), m_cd=b(m_cd), m_gk=b(m_gk), m_gk3=b(m_gk3), m_s=f(m_s), bones=b(m_s),
                m_st=f(m_st), mg=b(mg), sel=b(sel), cst=f(cst), expg=b(expg), expb=b(expb))


_AB_CONST_ORDER = ("ltri", "eye_w", "causal_w", "strict_w", "dmask", "m_cc", "m_cd", "m_gk", "m_gk3", "m_s",
                   "bones", "m_st", "mg", "sel", "cst", "expg", "expb")


def _ab_kernel(proj_ref, conv0_ref, dn0_ref, gl0_ref, convw_ref, gparam_ref, w2_ref, gkb_ref, onorm_ref,
               ltri_ref, eye_ref, causal_ref, strict_ref, dmask_ref, mcc_ref, mcd_ref, mgk_ref, mgk3_ref,
               ms_ref, bones_ref, mst_ref, mg_ref, sel_ref, cst_ref, expg_ref, expb_ref,
               o_ref, convn_ref, dns_ref, gls_ref,
               xp_ref, sdn_ref, sgl_ref, *, c, nc):
    i = pl.program_id(1)
    nsb = c // GLA_SUB

    @pl.when(i == 0)
    def _load_state():
        xp_ref[0:SUBLANE, :] = conv0_ref[...]
        sdn_ref[...] = dn0_ref[...]
        sgl_ref[...] = gl0_ref[...]

    x = proj_ref[:, OFF_QKV:OFF_QKV + DN_QKV]
    xp_ref[SUBLANE:SUBLANE + c, :] = x
    cw = convw_ref[...]
    y = xp_ref[SUBLANE - 3:SUBLANE - 3 + c, :] * cw[0:1, :]
    y = y + xp_ref[SUBLANE - 2:SUBLANE - 2 + c, :] * cw[1:2, :]
    y = y + xp_ref[SUBLANE - 1:SUBLANE - 1 + c, :] * cw[2:3, :]
    y = y + x * cw[3:4, :]
    xp_ref[0:SUBLANE, :] = x[c - SUBLANE:c, :]
    qkv = y * jax.nn.sigmoid(y)
    q = qkv[:, 0:HW]
    k = qkv[:, HW:2 * HW]
    v = qkv[:, 2 * HW:3 * HW]

    bones = bones_ref[...]
    mcd = mcd_ref[...]
    mcc = mcc_ref[...]

    ss = _mmx(jnp.concatenate([q * q, k * k], axis=0), bones, 2)
    q = q * lax.rsqrt(ss[0:c] + EPS) * (DN_DK ** -0.5)
    k = k * lax.rsqrt(ss[c:2 * c] + EPS)

    small = proj_ref[:, OFF_SMALL:OFF_SMALL + LANE]
    gp = gparam_ref[...]
    g_full = -jnp.exp(gp[0:1, :]) * _softplus(small + gp[1:2, :])
    beta_full = jax.nn.sigmoid(small)
    ltri = ltri_ref[...]
    gcum_parts = _split(_mmx_left(ltri, g_full, 3), 3)
    gcum_w = None
    g1 = cst_ref[0:1, :]
    g2 = cst_ref[1:2, :]
    for p in range(3):
        t = _mm(gcum_parts[p], expg_ref[...])
        gcum_w = t if gcum_w is None else gcum_w + t
        g1 = g1 + _mm(gcum_parts[p], sel_ref[p])
        g2 = g2 + _mm(gcum_parts[p], sel_ref[3 + p])
    beta_w = _mmx(beta_full, expb_ref[...], 2)
    glast_w = gcum_w[c - 1:c, :]
    eg_w = jnp.exp(gcum_w)
    edec_w = jnp.exp(glast_w - gcum_w)
    eglast_w = jnp.exp(glast_w)

    g2_blk = _tile_rows(g2.astype(BF16), H) * mg_ref[...]
    d_w = _mm_nt(g1.astype(BF16), g2_blk)
    decay_w = jnp.exp(jnp.where(causal_ref[...] > 0.0, d_w, -jnp.inf))

    kb = k * beta_w
    k_blk = _tile_rows(k.astype(BF16), H) * mcd
    r = _mm_nt(jnp.concatenate([kb, q], axis=0).astype(BF16), k_blk)
    a_w = strict_ref[...] * r[0:c] * decay_w
    qkd_w = r[c:2 * c] * decay_w

    def bd(y_w):
        return _tile_rows(y_w.astype(BF16), H) * mcc

    nlev = int(math.log2(c))
    pw = -a_w
    t_w = eye_ref[...] + pw
    pw = _mm(pw.astype(BF16), bd(pw))
    for _ in range(2, nlev):
        rr = _mm(jnp.concatenate([pw, t_w], axis=0).astype(BF16), bd(pw))
        t_w = t_w + rr[c:2 * c]
        pw = rr[0:c]
    t_w = t_w + _mm(t_w.astype(BF16), bd(pw))

    tb = t_w.astype(BF16)
    u = _mm(tb, _tile_rows((v * beta_w).astype(BF16), H) * mcd)
    wk = _mm(tb, _tile_rows((kb * eg_w).astype(BF16), H) * mcd)

    s_dn = sdn_ref[...]
    rs = _mm(jnp.concatenate([wk, q * eg_w], axis=0).astype(BF16), s_dn.astype(BF16))
    v_new = u - rs[0:c]
    vnb = v_new.astype(BF16)
    o_dn = rs[c:2 * c] + _mm(qkd_w.astype(BF16), _tile_rows(vnb, H) * mcd)
    sdn_ref[...] = s_dn * eglast_w + _mm_tn((k * edec_w).astype(BF16), vnb) * ms_ref[...]

    q2 = proj_ref[:, OFF_GQ:OFF_GQ + GW] * (GLA_DK ** -0.5)
    k2 = proj_ref[:, OFF_GK:OFF_GK + GW]
    v2 = proj_ref[:, OFF_GV:OFF_GV + HW]
    z = _mm(small.astype(BF16), w2_ref[...]) + gkb_ref[...]
    glog = -_softplus(-z) * (1.0 / GLA_NORMALIZER)
    bcum = _mmx_left(ltri, glog, 3)
    blast = bcum[c - 1:c, :]
    qe = q2 * jnp.exp(bcum)
    kdec = k2 * jnp.exp(blast - bcum)
    s_gl = sgl_ref[...]
    o_gl = _mm_nt(qe.astype(BF16), s_gl.astype(BF16))

    mgk = mgk_ref[...]
    rmid = jnp.concatenate(
        [jnp.broadcast_to(bcum[GLA_SUB * s + GLA_SUB // 2:GLA_SUB * s + GLA_SUB // 2 + 1, :], (GLA_SUB, GW))
         for s in range(nsb)], axis=0)
    qm = q2 * jnp.exp(bcum - rmid)
    km = k2 * jnp.exp(rmid - bcum)
    a_gl = dmask_ref[...] * _mm_nt(qm.astype(BF16), _tile_rows(km.astype(BF16), H) * mgk)
    if nsb > 1:
        rend = jnp.concatenate(
            [jnp.broadcast_to(bcum[GLA_SUB * (s + 1) - 1:GLA_SUB * (s + 1), :], (GLA_SUB, GW))
             for s in range(nsb)], axis=0)
        kr = (k2 * jnp.exp(rend - bcum)).astype(BF16)
        rowi = lax.broadcasted_iota(jnp.int32, (c, GW), 0)
        qs = []
        for s in range(nsb - 1):
            e = jnp.where(rowi >= GLA_SUB * (s + 1), bcum - rend[GLA_SUB * s:GLA_SUB * s + 1, :], -jnp.inf)
            qs.append((q2 * jnp.exp(e)).astype(BF16))
        qcat = jnp.concatenate(qs, axis=1)
        kcat = _tile_rows(jnp.concatenate([kr] * (nsb - 1), axis=1), H) * mgk3_ref[...]
        a_gl = a_gl + _mm_nt(qcat, kcat)
    v2b = v2.astype(BF16)
    o_gl = o_gl + _mm(a_gl.astype(BF16), _tile_rows(v2b, H) * mcd)
    sgl_ref[...] = s_gl * jnp.exp(blast) + _mm_tn(v2b, kdec.astype(BF16)) * mst_ref[...]

    ms = _mmx(jnp.concatenate([o_dn * o_dn, o_gl * o_gl], axis=0), bones, 2) * (1.0 / DN_DV)
    gd = proj_ref[:, OFF_DGATE:OFF_DGATE + HW]
    gg = proj_ref[:, OFF_GGATE:OFF_GGATE + HW]
    onorm = onorm_ref[...]
    y_dn = (o_dn * lax.rsqrt(ms[0:c] + EPS)) * onorm[0:1, :] * (gd * jax.nn.sigmoid(gd))
    y_gl = (o_gl * lax.rsqrt(ms[c:2 * c] + EPS)) * onorm[1:2, :] * (gg * jax.nn.sigmoid(gg))
    o_ref[:, 0:HW] = y_dn.astype(BF16)
    o_ref[:, HW:2 * HW] = y_gl.astype(BF16)

    @pl.when(i == nc - 1)
    def _store_state():
        convn_ref[...] = xp_ref[0:SUBLANE, :]
        dns_ref[...] = sdn_ref[...]
        gls_ref[...] = sgl_ref[...]


def _ab_mixer(proj, conv0, dn0, gl0, convw, gparam, w2p, gkb, onorm, c):
    b, l, _ = proj.shape
    nc = l // c
    consts = _ab_consts(c)
    cvals = [consts[n] for n in _AB_CONST_ORDER]
    per_b = lambda shape: pl.BlockSpec((None,) + shape, lambda bi, ci: (bi,) + (0,) * len(shape))
    in_specs = ([pl.BlockSpec((None, c, AB_PAD), lambda bi, ci: (bi, ci, 0)),
                 per_b((SUBLANE, DN_QKV)), per_b((HW, HW)), per_b((HW, GW)),
                 _const_spec(convw.shape), _const_spec(gparam.shape), _const_spec(w2p.shape),
                 _const_spec(gkb.shape), _const_spec(onorm.shape)]
                + [_const_spec(a.shape) for a in cvals])
    out_specs = [pl.BlockSpec((None, c, 2 * HW), lambda bi, ci: (bi, ci, 0)),
                 per_b((SUBLANE, DN_QKV)), per_b((HW, HW)), per_b((HW, GW))]
    out_shape = [jax.ShapeDtypeStruct((b, l, 2 * HW), BF16),
                 jax.ShapeDtypeStruct((b, SUBLANE, DN_QKV), F32),
                 jax.ShapeDtypeStruct((b, HW, HW), F32),
                 jax.ShapeDtypeStruct((b, HW, GW), F32)]
    return pl.pallas_call(
        functools.partial(_ab_kernel, c=c, nc=nc),
        grid=(b, nc),
        in_specs=in_specs,
        out_specs=out_specs,
        out_shape=out_shape,
        scratch_shapes=[pltpu.VMEM((SUBLANE + c, DN_QKV), F32),
                        pltpu.VMEM((HW, HW), F32),
                        pltpu.VMEM((HW, GW), F32)],
        compiler_params=pltpu.CompilerParams(dimension_semantics=("parallel", "arbitrary"),
                                             vmem_limit_bytes=VMEM_LIMIT),
        name="ab_mixer",
    )(proj, conv0, dn0, gl0, convw, gparam, w2p, gkb, onorm, *cvals)


def _softmax_rows(parts):
    m = None
    for s in parts:
        t = jnp.max(s, axis=-1, keepdims=True)
        m = t if m is None else jnp.maximum(m, t)
    es = [jnp.exp(s - m) for s in parts]
    l = None
    for e in es:
        t = jnp.sum(e, axis=-1, keepdims=True)
        l = t if l is None else l + t
    inv = 1.0 / l
    return [e * inv for e in es]


def _attn_prompt_kernel(q_ref, k0_ref, k1_ref, k2_ref, v0_ref, v1_ref, v2_ref, bias_ref, o_ref, *, qb):
    i = pl.program_id(1)
    col = lax.broadcasted_iota(jnp.int32, (qb, 3 * qb), 1)
    valid = (col + (i - 2) * qb) >= 0
    scale = C_HD ** -0.5
    for h in range(C_HEADS):
        sl = slice(C_HD * h, C_HD * (h + 1))
        qh = q_ref[:, sl].astype(BF16)
        kh = jnp.concatenate([k0_ref[:, sl], k1_ref[:, sl], k2_ref[:, sl]], axis=0).astype(BF16)
        vh = jnp.concatenate([v0_ref[:, sl], v1_ref[:, sl], v2_ref[:, sl]], axis=0).astype(BF16)
        s = _mm_nt(qh, kh) * scale + bias_ref[h]
        s = jnp.where(valid, s, -jnp.inf)
        (p,) = _softmax_rows([s])
        o_ref[:, sl] = _mm(p.astype(BF16), vh).astype(BF16)


def _attn_prompt(qkv, bias, qb):
    b, l, _ = qkv.shape
    dm = C_HEADS * C_HD
    kv_spec = lambda colblk, back: pl.BlockSpec(
        (None, qb, dm), lambda bi, i: (bi, jnp.maximum(i - back, 0), colblk))
    return pl.pallas_call(
        functools.partial(_attn_prompt_kernel, qb=qb),
        grid=(b, l // qb),
        in_specs=[pl.BlockSpec((None, qb, dm), lambda bi, i: (bi, i, 0)),
                  kv_spec(1, 2), kv_spec(1, 1), kv_spec(1, 0),
                  kv_spec(2, 2), kv_spec(2, 1), kv_spec(2, 0),
                  _const_spec(bias.shape)],
        out_specs=pl.BlockSpec((None, qb, dm), lambda bi, i: (bi, i, 0)),
        out_shape=jax.ShapeDtypeStruct((b, l, dm), BF16),
        compiler_params=pltpu.CompilerParams(dimension_semantics=("parallel", "parallel"),
                                             vmem_limit_bytes=VMEM_LIMIT),
        name="attn_prompt",
    )(qkv, qkv, qkv, qkv, qkv, qkv, qkv, bias)


def _attn_sample_kernel(q_ref, k_ref, v_ref, ck_ref, cv_ref, bc_ref, bn_ref, o_ref):
    scale = C_HD ** -0.5
    for h in range(C_HEADS):
        sl = slice(C_HD * h, C_HD * (h + 1))
        qh = q_ref[:, sl].astype(BF16)
        sc = _mm_nt(qh, ck_ref[:, sl].astype(BF16)) * scale + bc_ref[h]
        sn = _mm_nt(qh, k_ref[:, sl].astype(BF16)) * scale + bn_ref[h]
        pc, pn = _softmax_rows([sc, sn])
        oh = _mm(pc.astype(BF16), cv_ref[:, sl].astype(BF16)) + _mm(pn.astype(BF16), v_ref[:, sl].astype(BF16))
        o_ref[:, sl] = oh.astype(BF16)


def _attn_sample(qkv, ck, cv, bias_c, bias_n):
    b, l, _ = qkv.shape
    dm = C_HEADS * C_HD
    ncache = ck.shape[1]
    new_spec = lambda colblk: pl.BlockSpec((None, l, dm), lambda bi: (bi, 0, colblk))
    cache_spec = pl.BlockSpec((None, ncache, dm), lambda bi: (bi, 0, 0))
    return pl.pallas_call(
        _attn_sample_kernel,
        grid=(b,),
        in_specs=[new_spec(0), new_spec(1), new_spec(2), cache_spec, cache_spec,
                  _const_spec(bias_c.shape), _const_spec(bias_n.shape)],
        out_specs=pl.BlockSpec((None, l, dm), lambda bi: (bi, 0, 0)),
        out_shape=jax.ShapeDtypeStruct((b, l, dm), BF16),
        compiler_params=pltpu.CompilerParams(dimension_semantics=("parallel",),
                                             vmem_limit_bytes=VMEM_LIMIT),
        name="attn_sample",
    )(qkv, qkv, qkv, ck, cv, bias_c, bias_n)


def _prompt_bias(table, qb):
    rq = np.arange(qb)[:, None]
    rk = np.arange(3 * qb)[None, :]
    rel = np.clip(rq + 2 * qb - rk, -MAX_REL, MAX_REL) + MAX_REL
    qc = rq // CHUNK
    kc = rk // CHUNK - (2 * qb) // CHUNK
    band = (kc <= qc) & (kc >= qc - C_BAND_CHUNKS)
    bias = table[:, rel].astype(F32)
    return jnp.where(jnp.asarray(band)[None], bias, -jnp.inf)


def _sample_bias(table, l, ncache):
    qpos = PAST_LEN + np.arange(l)
    kpos = np.concatenate([PAST_LEN - ncache + np.arange(ncache), PAST_LEN + np.arange(l)])
    rel = np.clip(qpos[:, None] - kpos[None, :], -MAX_REL, MAX_REL) + MAX_REL
    qch = qpos // CHUNK
    kch = kpos // CHUNK
    valid = (kch[None, :] <= qch[:, None]) & (kch[None, :] >= qch[:, None] - C_BAND_CHUNKS)
    bias = jnp.where(jnp.asarray(valid)[None], table[:, rel].astype(F32), -jnp.inf)
    return bias[:, :, :ncache], bias[:, :, ncache:]


def _permute_w_in(w):
    d = w.shape[0]
    o_a = DN_QKV
    o_b = o_a + DN_HEADS
    o_gate = o_b + DN_HEADS
    o_gq = o_gate + HW
    o_gk = o_gq + GW
    o_gv = o_gk + GW
    o_lr = o_gv + HW
    o_gg = o_lr + GLA_RANK
    return jnp.concatenate(
        [w[:, :DN_QKV], w[:, o_gate:o_gq], w[:, o_gq:o_gk], w[:, o_gk:o_gv], w[:, o_gv:o_lr],
         w[:, o_gg:o_gg + HW], w[:, o_a:o_gate], w[:, o_lr:o_gg],
         jnp.zeros((d, LANE - 2 * DN_HEADS - GLA_RANK), w.dtype)], axis=1)


def _diag_blocks(s, nh, rows, cols):
    return jnp.stack([s[:, rows * h:rows * (h + 1), cols * h:cols * (h + 1)] for h in range(nh)], axis=1)


def _embed_blocks(s):
    b, nh, rows, cols = s.shape
    eye = jnp.eye(nh, dtype=s.dtype)
    return (s[:, :, :, None, :] * eye[None, :, None, :, None]).reshape(b, nh * rows, nh * cols)


def _trunk(x, conv0, dn0, gla0, ck, cv, w, prompt, tm):
    b, l, d = x.shape
    depth = w["ffn_norm"].shape[0]
    c = min(CHUNK, l)
    xf = x.reshape(b * l, d)
    convs, dns, glas, ks, vs = [], [], [], [], []
    for layer in range(depth):
        i = layer // 2
        if layer % 2 == 0:
            proj = _pre(xf, w["ab_norm"][i], w["ab_w_in"][i], tm).reshape(b, l, AB_PAD)
            conv0p = jnp.pad(conv0[i], ((0, 0), (SUBLANE - (CONV_W - 1), 0), (0, 0)))
            dn0bd = _embed_blocks(dn0[i])
            gl0bd = _embed_blocks(jnp.swapaxes(gla0[i], -1, -2))
            o, convn, dn_s, gl_s = _ab_mixer(proj, conv0p, dn0bd, gl0bd, w["dn_conv_w"][i], w["gparam"][i],
                                             w["gk_w2p"][i], w["gla_gk_b"][i], w["onorm"][i], c)
            convs.append(convn[:, SUBLANE - (CONV_W - 1):, :])
            dns.append(_diag_blocks(dn_s, DN_HEADS, DN_DK, DN_DV))
            glas.append(jnp.swapaxes(_diag_blocks(gl_s, GLA_HEADS, GLA_DV, GLA_DK), -1, -2))
            wo = w["ab_w_out"][i]
        else:
            qkv = _pre(xf, w["c_norm"][i], w["c_w_qkv"][i], tm).reshape(b, l, 3 * C_HEADS * C_HD)
            dm = C_HEADS * C_HD
            if prompt:
                o = _attn_prompt(qkv, _prompt_bias(w["c_rel_bias"][i], ATT_QB), ATT_QB)
                rows = min(C_PAST, l)
                ks.append(qkv[:, l - rows:, dm:2 * dm].reshape(b, rows, C_HEADS, C_HD))
                vs.append(qkv[:, l - rows:, 2 * dm:].reshape(b, rows, C_HEADS, C_HD))
            else:
                ncache = ck.shape[2]
                bias_c, bias_n = _sample_bias(w["c_rel_bias"][i], l, ncache)
                o = _attn_sample(qkv, ck[i].reshape(b, ncache, dm), cv[i].reshape(b, ncache, dm), bias_c, bias_n)
                ks.append(qkv[:, :, dm:2 * dm].reshape(b, l, C_HEADS, C_HD))
                vs.append(qkv[:, :, 2 * dm:].reshape(b, l, C_HEADS, C_HD))
            wo = w["c_w_out"][i]
        xf = _post(o.reshape(b * l, -1), xf, wo, w["ffn_norm"][layer], w["ffn_w_gu"][layer],
                   w["ffn_w_down"][layer], w["final_norm"], tm, final=(layer == depth - 1))
    return (xf.reshape(b, l, d), jnp.stack(convs), jnp.stack(dns), jnp.stack(glas), jnp.stack(ks), jnp.stack(vs))


def kernel(x_prompt, x_sample, state_dn_conv, state_dn, state_gla, cache_c_k, cache_c_v, ab_norm, ab_w_in, dn_conv_w, dn_a_log, dn_dt_bias, dn_out_norm, gla_gk_w2, gla_gk_b, gla_out_norm, ab_w_out, c_norm, c_w_qkv, c_rel_bias, c_w_out, ffn_norm, ffn_w_gu, ffn_w_down, final_norm):
    n_ab = ab_w_in.shape[0]
    bsz = x_prompt.shape[0]
    gparam = jnp.zeros((n_ab, SUBLANE, LANE), F32)
    gparam = gparam.at[:, 0, :DN_HEADS].set(dn_a_log).at[:, 1, :DN_HEADS].set(dn_dt_bias)
    gk_w2p = jnp.zeros((n_ab, LANE, GW), F32).at[:, 2 * DN_HEADS:2 * DN_HEADS + GLA_RANK, :].set(gla_gk_w2)
    onorm = jnp.stack([jnp.tile(dn_out_norm, (1, DN_HEADS)), jnp.tile(gla_out_norm, (1, GLA_HEADS))], axis=1)
    w = {
        "ab_norm": ab_norm, "c_norm": c_norm, "ffn_norm": ffn_norm, "final_norm": final_norm,
        "ab_w_in": jnp.stack([_permute_w_in(ab_w_in[i]) for i in range(n_ab)]).astype(BF16),
        "dn_conv_w": dn_conv_w, "gparam": gparam, "gk_w2p": gk_w2p.astype(BF16),
        "gla_gk_b": gla_gk_b[:, None, :], "onorm": onorm,
        "ab_w_out": ab_w_out.astype(BF16), "c_w_qkv": c_w_qkv.astype(BF16), "c_rel_bias": c_rel_bias,
        "c_w_out": c_w_out.astype(BF16), "ffn_w_gu": ffn_w_gu.astype(BF16), "ffn_w_down": ffn_w_down.astype(BF16),
    }
    conv0 = jnp.zeros((n_ab, bsz, CONV_W - 1, DN_QKV), F32)
    dn0 = jnp.zeros((n_ab, bsz, DN_HEADS, DN_DK, DN_DV), F32)
    gla0 = jnp.zeros((n_ab, bsz, GLA_HEADS, GLA_DK, GLA_DV), F32)
    y_p, conv_p, dn_p, gla_p, ck_p, cv_p = _trunk(x_prompt, conv0, dn0, gla0, None, None, w, True, 512)
    ts = x_sample.shape[0] * x_sample.shape[1]
    y_s, conv_s, dn_s, gla_s, ck_s, cv_s = _trunk(x_sample, state_dn_conv, state_dn, state_gla,
                                                  cache_c_k, cache_c_v, w, False, ts)
    return (y_p, y_s, conv_p, conv_s, dn_p, dn_s, gla_p, gla_s, ck_p, ck_s, cv_p, cv_s)
```

```python
import functools
import math

import numpy as np
import jax
import jax.numpy as jnp
from jax import lax
from jax.experimental import pallas as pl
from jax.experimental.pallas import tpu as pltpu

F32 = jnp.float32
BF16 = jnp.bfloat16

EPS = 1e-6
CHUNK = 64
PAST_LEN = 2048
D_MODEL = 1024
DN_HEADS = 8
DN_DK = 64
DN_DV = 64
CONV_W = 4
DN_QKV = DN_HEADS * (2 * DN_DK + DN_DV)
GLA_HEADS = 8
GLA_DK = 32
GLA_DV = 64
GLA_RANK = 16
GLA_NORMALIZER = 16.0
GLA_SUB = 16
C_HEADS = 16
C_HD = 64
C_BAND_CHUNKS = 8
C_PAST = C_BAND_CHUNKS * CHUNK
MAX_REL = 128
ATT_QB = 256
LOG2E = math.log2(math.e)
ATT_QSCALE = C_HD ** -0.5 * LOG2E

H = DN_HEADS
HW = DN_HEADS * DN_DV
GW = GLA_HEADS * GLA_DK
LANE = 128
SUBLANE = 8

OFF_QKV = 0
OFF_DGATE = DN_QKV
OFF_GQ = OFF_DGATE + HW
OFF_GK = OFF_GQ + GW
OFF_GV = OFF_GK + GW
OFF_GGATE = OFF_GV + HW
OFF_SMALL = OFF_GGATE + HW
AB_PAD = OFF_SMALL + LANE

VMEM_LIMIT = 56 * 1024 * 1024


def _mm(a, b):
    return jnp.dot(a, b, preferred_element_type=F32)


def _mm_nt(a, b):
    return lax.dot_general(a, b, (((1,), (1,)), ((), ())), preferred_element_type=F32)


def _mm_tn(a, b):
    return lax.dot_general(a, b, (((0,), (0,)), ((), ())), preferred_element_type=F32)


def _split(x, n):
    parts = []
    r = x
    for t in range(n):
        p = r.astype(BF16)
        parts.append(p)
        if t + 1 < n:
            r = r - p.astype(F32)
    return parts


def _mmx(x, m, n):
    out = None
    for p in _split(x, n):
        t = _mm(p, m)
        out = t if out is None else out + t
    return out


def _mmx_left(m, x, n):
    out = None
    for p in _split(x, n):
        t = _mm(m, p)
        out = t if out is None else out + t
    return out


def _softplus(x):
    return jnp.maximum(x, 0.0) + jnp.log(1.0 + jnp.exp(-jnp.abs(x)))


def _tile_rows(x, n):
    return jnp.concatenate([x] * n, axis=0)


def _rms(x, g):
    ms = jnp.mean(x * x, axis=-1, keepdims=True)
    return (x * lax.rsqrt(ms + EPS)) * g


def _const_spec(shape):
    nd = len(shape)
    return pl.BlockSpec(shape, lambda *_: (0,) * nd, pipeline_mode=pl.Buffered(1))


def _pre_kernel(x_ref, g_ref, w_ref, o_ref, *, col_chunk):
    xn = _rms(x_ref[...], g_ref[...]).astype(BF16)
    n = o_ref.shape[-1]
    for c0 in range(0, n, col_chunk):
        c1 = min(n, c0 + col_chunk)
        o_ref[:, c0:c1] = _mm(xn, w_ref[:, c0:c1])


def _pre(x, g, w, tm):
    t, d = x.shape
    n = w.shape[1]
    return pl.pallas_call(
        functools.partial(_pre_kernel, col_chunk=512),
        grid=(t // tm,),
        in_specs=[pl.BlockSpec((tm, d), lambda i: (i, 0)),
                  _const_spec((1, d)),
                  _const_spec((d, n))],
        out_specs=pl.BlockSpec((tm, n), lambda i: (i, 0)),
        out_shape=jax.ShapeDtypeStruct((t, n), F32),
        compiler_params=pltpu.CompilerParams(dimension_semantics=("parallel",),
                                             vmem_limit_bytes=VMEM_LIMIT),
        name="pre_proj",
    )(x, g.reshape(1, d), w)


def _post_kernel(o_ref, x_ref, wo_ref, g_ref, wgu_ref, wd_ref, gf_ref, out_ref, acc_ref, *, ff_chunk, final):
    x1 = x_ref[...] + _mm(o_ref[...], wo_ref[...])
    xn = _rms(x1, g_ref[...]).astype(BF16)
    dff = wd_ref.shape[0]
    for f0 in range(0, dff, ff_chunk):
        gate = _mm(xn, wgu_ref[:, f0:f0 + ff_chunk])
        up = _mm(xn, wgu_ref[:, dff + f0:dff + f0 + ff_chunk])
        a = (gate * jax.nn.sigmoid(gate) * up).astype(BF16)
        part = _mm(a, wd_ref[f0:f0 + ff_chunk, :])
        if f0 == 0:
            acc_ref[...] = part
        else:
            acc_ref[...] += part
    x2 = x1 + acc_ref[...]
    if final:
        x2 = _rms(x2, gf_ref[...])
    out_ref[...] = x2


def _post(o, x, wo, g, wgu, wd, gf, tm, final):
    t, d = x.shape
    dm = o.shape[1]
    dff = wd.shape[0]
    return pl.pallas_call(
        functools.partial(_post_kernel, ff_chunk=256, final=final),
        grid=(t // tm,),
        in_specs=[pl.BlockSpec((tm, dm), lambda i: (i, 0)),
                  pl.BlockSpec((tm, d), lambda i: (i, 0)),
                  _const_spec((dm, d)),
                  _const_spec((1, d)),
                  _const_spec((d, 2 * dff)),
                  _const_spec((dff, d)),
                  _const_spec((1, d))],
        out_specs=pl.BlockSpec((tm, d), lambda i: (i, 0)),
        out_shape=jax.ShapeDtypeStruct((t, d), F32),
        scratch_shapes=[pltpu.VMEM((tm, d), F32)],
        compiler_params=pltpu.CompilerParams(dimension_semantics=("parallel",),
                                             vmem_limit_bytes=VMEM_LIMIT),
        name="post_ffn",
    )(o, x, wo, g.reshape(1, d), wgu, wd, gf.reshape(1, d))


@functools.lru_cache(maxsize=None)
def _ab_consts(c):
    w = H * c
    nsb = c // GLA_SUB
    i = np.arange(c)
    hw_head = np.arange(HW) // DN_DV
    gw_head = np.arange(GW) // GLA_DK
    w_head = np.arange(w) // c
    w_pos = np.arange(w) % c
    ltri = (i[None, :] <= i[:, None])
    eye_w = (i[:, None] == w_pos[None, :])
    causal_w = (w_pos[None, :] <= i[:, None])
    strict_w = (w_pos[None, :] < i[:, None])
    dmask = causal_w & ((i[:, None] // GLA_SUB) == (w_pos[None, :] // GLA_SUB))
    m_cc = (w_head[:, None] == w_head[None, :])
    m_cd = (w_head[:, None] == hw_head[None, :])
    m_gk = (w_head[:, None] == gw_head[None, :])
    m_gk3 = np.concatenate([m_gk & ((w_pos // GLA_SUB) == j)[:, None] for j in range(max(nsb - 1, 1))], axis=1)
    m_s = (hw_head[:, None] == hw_head[None, :])
    m_st = (hw_head[:, None] == gw_head[None, :])
    mg = (w_head[:, None] == (np.arange(LANE) // 16)[None, :])
    sel = np.zeros((6, LANE, LANE), np.float32)
    cst = np.zeros((SUBLANE, LANE), np.float32)
    for h in range(H):
        for p in range(3):
            sel[p, h, 16 * h + p] = 1.0
            sel[3 + p, h, 16 * h + 3 + p] = -1.0
            cst[0, 16 * h + 3 + p] = 1.0
            cst[1, 16 * h + p] = 1.0
    expg = np.zeros((LANE, HW), np.float32)
    expb = np.zeros((LANE, HW), np.float32)
    for h in range(H):
        expg[h, DN_DV * h:DN_DV * (h + 1)] = 1.0
        expb[H + h, DN_DV * h:DN_DV * (h + 1)] = 1.0
    f = lambda a: jnp.asarray(a, F32)
    b = lambda a: jnp.asarray(a, BF16)
    return dict(ltri=b(ltri), eye_w=f(eye_w), causal_w=f(causal_w), strict_w=f(strict_w), dmask=f(dmask),
                m_cc=b(m_cc), m_cd=b(m_cd), m_gk=b(m_gk), m_gk3=b(m_gk3), m_s=f(m_s), bones=b(m_s),
                m_st=f(m_st), mg=b(mg), sel=b(sel), cst=f(cst), expg=b(expg), expb=b(expb))


_AB_CONST_ORDER = ("ltri", "eye_w", "causal_w", "strict_w", "dmask", "m_cc", "m_cd", "m_gk", "m_gk3", "m_s",
                   "bones", "m_st", "mg", "sel", "cst", "expg", "expb")


def _ab_kernel(proj_ref, conv0_ref, dn0_ref, gl0_ref, convw_ref, gparam_ref, w2_ref, gkb_ref, onorm_ref,
               ltri_ref, eye_ref, causal_ref, strict_ref, dmask_ref, mcc_ref, mcd_ref, mgk_ref, mgk3_ref,
               ms_ref, bones_ref, mst_ref, mg_ref, sel_ref, cst_ref, expg_ref, expb_ref,
               o_ref, convn_ref, dns_ref, gls_ref,
               xp_ref, sdn_ref, sgl_ref, *, c, nc):
    i = pl.program_id(1)
    nsb = c // GLA_SUB

    @pl.when(i == 0)
    def _load_state():
        xp_ref[0:SUBLANE, :] = conv0_ref[...]
        sdn_ref[...] = dn0_ref[...]
        sgl_ref[...] = gl0_ref[...]

    x = proj_ref[:, OFF_QKV:OFF_QKV + DN_QKV]
    xp_ref[SUBLANE:SUBLANE + c, :] = x
    cw = convw_ref[...]
    y = xp_ref[SUBLANE - 3:SUBLANE - 3 + c, :] * cw[0:1, :]
    y = y + xp_ref[SUBLANE - 2:SUBLANE - 2 + c, :] * cw[1:2, :]
    y = y + xp_ref[SUBLANE - 1:SUBLANE - 1 + c, :] * cw[2:3, :]
    y = y + x * cw[3:4, :]
    xp_ref[0:SUBLANE, :] = x[c - SUBLANE:c, :]
    qkv = y * jax.nn.sigmoid(y)
    q = qkv[:, 0:HW]
    k = qkv[:, HW:2 * HW]
    v = qkv[:, 2 * HW:3 * HW]

    bones = bones_ref[...]
    mcd = mcd_ref[...]
    mcc = mcc_ref[...]

    ss = _mmx(jnp.concatenate([q * q, k * k], axis=0), bones, 2)
    q = q * lax.rsqrt(ss[0:c] + EPS) * (DN_DK ** -0.5)
    k = k * lax.rsqrt(ss[c:2 * c] + EPS)

    small = proj_ref[:, OFF_SMALL:OFF_SMALL + LANE]
    gp = gparam_ref[...]
    g_full = -jnp.exp(gp[0:1, :]) * _softplus(small + gp[1:2, :])
    beta_full = jax.nn.sigmoid(small)
    ltri = ltri_ref[...]
    gcum_parts = _split(_mmx_left(ltri, g_full, 3), 3)
    gcum_w = None
    g1 = cst_ref[0:1, :]
    g2 = cst_ref[1:2, :]
    for p in range(3):
        t = _mm(gcum_parts[p], expg_ref[...])
        gcum_w = t if gcum_w is None else gcum_w + t
        g1 = g1 + _mm(gcum_parts[p], sel_ref[p])
        g2 = g2 + _mm(gcum_parts[p], sel_ref[3 + p])
    beta_w = _mmx(beta_full, expb_ref[...], 2)
    glast_w = gcum_w[c - 1:c, :]
    eg_w = jnp.exp(gcum_w)
    edec_w = jnp.exp(glast_w - gcum_w)
    eglast_w = jnp.exp(glast_w)

    g2_blk = _tile_rows(g2.astype(BF16), H) * mg_ref[...]
    d_w = _mm_nt(g1.astype(BF16), g2_blk)
    decay_w = jnp.exp(jnp.where(causal_ref[...] > 0.0, d_w, -jnp.inf))

    kb = k * beta_w
    k_blk = _tile_rows(k.astype(BF16), H) * mcd
    r = _mm_nt(jnp.concatenate([kb, q], axis=0).astype(BF16), k_blk)
    a_w = strict_ref[...] * r[0:c] * decay_w
    qkd_w = r[c:2 * c] * decay_w

    def bd(y_w):
        return _tile_rows(y_w.astype(BF16), H) * mcc

    nlev = int(math.log2(c))
    pw = -a_w
    t_w = eye_ref[...] + pw
    pw = _mm(pw.astype(BF16), bd(pw))
    for _ in range(2, nlev):
        rr = _mm(jnp.concatenate([pw, t_w], axis=0).astype(BF16), bd(pw))
        t_w = t_w + rr[c:2 * c]
        pw = rr[0:c]
    t_w = t_w + _mm(t_w.astype(BF16), bd(pw))

    tb = t_w.astype(BF16)
    u = _mm(tb, _tile_rows((v * beta_w).astype(BF16), H) * mcd)
    wk = _mm(tb, _tile_rows((kb * eg_w).astype(BF16), H) * mcd)

    s_dn = sdn_ref[...]
    rs = _mm(jnp.concatenate([wk, q * eg_w], axis=0).astype(BF16), s_dn.astype(BF16))
    v_new = u - rs[0:c]
    vnb = v_new.astype(BF16)
    o_dn = rs[c:2 * c] + _mm(qkd_w.astype(BF16), _tile_rows(vnb, H) * mcd)
    sdn_ref[...] = s_dn * eglast_w + _mm_tn((k * edec_w).astype(BF16), vnb) * ms_ref[...]

    q2 = proj_ref[:, OFF_GQ:OFF_GQ + GW] * (GLA_DK ** -0.5)
    k2 = proj_ref[:, OFF_GK:OFF_GK + GW]
    v2 = proj_ref[:, OFF_GV:OFF_GV + HW]
    z = _mm(small.astype(BF16), w2_ref[...]) + gkb_ref[...]
    glog = -_softplus(-z) * (1.0 / GLA_NORMALIZER)
    bcum = _mmx_left(ltri, glog, 3)
    blast = bcum[c - 1:c, :]
    qe = q2 * jnp.exp(bcum)
    kdec = k2 * jnp.exp(blast - bcum)
    s_gl = sgl_ref[...]
    o_gl = _mm_nt(qe.astype(BF16), s_gl.astype(BF16))

    mgk = mgk_ref[...]
    rmid = jnp.concatenate(
        [jnp.broadcast_to(bcum[GLA_SUB * s + GLA_SUB // 2:GLA_SUB * s + GLA_SUB // 2 + 1, :], (GLA_SUB, GW))
         for s in range(nsb)], axis=0)
    qm = q2 * jnp.exp(bcum - rmid)
    km = k2 * jnp.exp(rmid - bcum)
    a_gl = dmask_ref[...] * _mm_nt(qm.astype(BF16), _tile_rows(km.astype(BF16), H) * mgk)
    if nsb > 1:
        rend = jnp.concatenate(
            [jnp.broadcast_to(bcum[GLA_SUB * (s + 1) - 1:GLA_SUB * (s + 1), :], (GLA_SUB, GW))
             for s in range(nsb)], axis=0)
        kr = (k2 * jnp.exp(rend - bcum)).astype(BF16)
        rowi = lax.broadcasted_iota(jnp.int32, (c, GW), 0)
        qs = []
        for s in range(nsb - 1):
            e = jnp.where(rowi >= GLA_SUB * (s + 1), bcum - rend[GLA_SUB * s:GLA_SUB * s + 1, :], -jnp.inf)
            qs.append((q2 * jnp.exp(e)).astype(BF16))
        qcat = jnp.concatenate(qs, axis=1)
        kcat = _tile_rows(jnp.concatenate([kr] * (nsb - 1), axis=1), H) * mgk3_ref[...]
        a_gl = a_gl + _mm_nt(qcat, kcat)
    v2b = v2.astype(BF16)
    o_gl = o_gl + _mm(a_gl.astype(BF16), _tile_rows(v2b, H) * mcd)
    sgl_ref[...] = s_gl * jnp.exp(blast) + _mm_tn(v2b, kdec.astype(BF16)) * mst_ref[...]

    ms = _mmx(jnp.concatenate([o_dn * o_dn, o_gl * o_gl], axis=0), bones, 2) * (1.0 / DN_DV)
    gd = proj_ref[:, OFF_DGATE:OFF_DGATE + HW]
    gg = proj_ref[:, OFF_GGATE:OFF_GGATE + HW]
    onorm = onorm_ref[...]
    y_dn = (o_dn * lax.rsqrt(ms[0:c] + EPS)) * onorm[0:1, :] * (gd * jax.nn.sigmoid(gd))
    y_gl = (o_gl * lax.rsqrt(ms[c:2 * c] + EPS)) * onorm[1:2, :] * (gg * jax.nn.sigmoid(gg))
    o_ref[:, 0:HW] = y_dn.astype(BF16)
    o_ref[:, HW:2 * HW] = y_gl.astype(BF16)

    @pl.when(i == nc - 1)
    def _store_state():
        convn_ref[...] = xp_ref[0:SUBLANE, :]
        dns_ref[...] = sdn_ref[...]
        gls_ref[...] = sgl_ref[...]


def _ab_mixer(proj, conv0, dn0, gl0, convw, gparam, w2p, gkb, onorm, c):
    b, l, _ = proj.shape
    nc = l // c
    consts = _ab_consts(c)
    cvals = [consts[n] for n in _AB_CONST_ORDER]
    per_b = lambda shape: pl.BlockSpec((None,) + shape, lambda bi, ci: (bi,) + (0,) * len(shape))
    in_specs = ([pl.BlockSpec((None, c, AB_PAD), lambda bi, ci: (bi, ci, 0)),
                 per_b((SUBLANE, DN_QKV)), per_b((HW, HW)), per_b((HW, GW)),
                 _const_spec(convw.shape), _const_spec(gparam.shape), _const_spec(w2p.shape),
                 _const_spec(gkb.shape), _const_spec(onorm.shape)]
                + [_const_spec(a.shape) for a in cvals])
    out_specs = [pl.BlockSpec((None, c, 2 * HW), lambda bi, ci: (bi, ci, 0)),
                 per_b((SUBLANE, DN_QKV)), per_b((HW, HW)), per_b((HW, GW))]
    out_shape = [jax.ShapeDtypeStruct((b, l, 2 * HW), BF16),
                 jax.ShapeDtypeStruct((b, SUBLANE, DN_QKV), F32),
                 jax.ShapeDtypeStruct((b, HW, HW), F32),
                 jax.ShapeDtypeStruct((b, HW, GW), F32)]
    return pl.pallas_call(
        functools.partial(_ab_kernel, c=c, nc=nc),
        grid=(b, nc),
        in_specs=in_specs,
        out_specs=out_specs,
        out_shape=out_shape,
        scratch_shapes=[pltpu.VMEM((SUBLANE + c, DN_QKV), F32),
                        pltpu.VMEM((HW, HW), F32),
                        pltpu.VMEM((HW, GW), F32)],
        compiler_params=pltpu.CompilerParams(dimension_semantics=("parallel", "arbitrary"),
                                             vmem_limit_bytes=VMEM_LIMIT),
        name="ab_mixer",
    )(proj, conv0, dn0, gl0, convw, gparam, w2p, gkb, onorm, *cvals)


def _softmax_rows(parts):
    m = None
    for s in parts:
        t = jnp.max(s, axis=-1, keepdims=True)
        m = t if m is None else jnp.maximum(m, t)
    es = [jnp.exp(s - m) for s in parts]
    l = None
    for e in es:
        t = jnp.sum(e, axis=-1, keepdims=True)
        l = t if l is None else l + t
    inv = 1.0 / l
    return [e * inv for e in es]


def _fold_lanes(x, op):
    out = x[:, 0:LANE]
    for c0 in range(LANE, x.shape[1], LANE):
        out = op(out, x[:, c0:c0 + LANE])
    return out


def _attn_prompt_kernel(q_ref, k0_ref, k1_ref, k2_ref, v0_ref, v1_ref, v2_ref, bias_ref, o_ref, *, qb):
    i = pl.program_id(1)
    offs = (jnp.where(i >= 2, 0.0, -jnp.inf), jnp.where(i >= 1, 0.0, -jnp.inf), 0.0)
    k_refs = (k0_ref, k1_ref, k2_ref)
    v_refs = (v0_ref, v1_ref, v2_ref)
    for h in range(C_HEADS):
        sl = slice(C_HD * h, C_HD * (h + 1))
        qh = (q_ref[:, sl] * ATT_QSCALE).astype(BF16)
        ss, em = [], None
        for t in range(3):
            s = _mm_nt(qh, k_refs[t][:, sl].astype(BF16)) + bias_ref[h, :, t * qb:(t + 1) * qb]
            et = _fold_lanes(s, jnp.maximum) + offs[t]
            em = et if em is None else jnp.maximum(em, et)
            ss.append(s)
        m = jnp.max(em, axis=-1, keepdims=True)
        acc, el = None, None
        for t in range(3):
            p = jnp.exp2(ss[t] - (m - offs[t]))
            lt = _fold_lanes(p, jnp.add)
            at = _mm(p.astype(BF16), v_refs[t][:, sl].astype(BF16))
            el = lt if el is None else el + lt
            acc = at if acc is None else acc + at
        l = jnp.sum(el, axis=-1, keepdims=True)
        o_ref[:, sl] = (acc * (1.0 / l)).astype(BF16)


def _attn_prompt(qkv, bias, qb):
    b, l, _ = qkv.shape
    dm = C_HEADS * C_HD
    kv_spec = lambda colblk, back: pl.BlockSpec(
        (None, qb, dm), lambda bi, i: (bi, jnp.maximum(i - back, 0), colblk))
    return pl.pallas_call(
        functools.partial(_attn_prompt_kernel, qb=qb),
        grid=(b, l // qb),
        in_specs=[pl.BlockSpec((None, qb, dm), lambda bi, i: (bi, i, 0)),
                  kv_spec(1, 2), kv_spec(1, 1), kv_spec(1, 0),
                  kv_spec(2, 2), kv_spec(2, 1), kv_spec(2, 0),
                  _const_spec(bias.shape)],
        out_specs=pl.BlockSpec((None, qb, dm), lambda bi, i: (bi, i, 0)),
        out_shape=jax.ShapeDtypeStruct((b, l, dm), BF16),
        compiler_params=pltpu.CompilerParams(dimension_semantics=("parallel", "parallel"),
                                             vmem_limit_bytes=VMEM_LIMIT),
        name="attn_prompt",
    )(qkv, qkv, qkv, qkv, qkv, qkv, qkv, bias)


def _attn_sample_kernel(q_ref, k_ref, v_ref, ck_ref, cv_ref, bc_ref, bn_ref, o_ref):
    scale = C_HD ** -0.5
    for h in range(C_HEADS):
        sl = slice(C_HD * h, C_HD * (h + 1))
        qh = q_ref[:, sl].astype(BF16)
        sc = _mm_nt(qh, ck_ref[:, sl].astype(BF16)) * scale + bc_ref[h]
        sn = _mm_nt(qh, k_ref[:, sl].astype(BF16)) * scale + bn_ref[h]
        pc, pn = _softmax_rows([sc, sn])
        oh = _mm(pc.astype(BF16), cv_ref[:, sl].astype(BF16)) + _mm(pn.astype(BF16), v_ref[:, sl].astype(BF16))
        o_ref[:, sl] = oh.astype(BF16)


def _attn_sample(qkv, ck, cv, bias_c, bias_n):
    b, l, _ = qkv.shape
    dm = C_HEADS * C_HD
    ncache = ck.shape[1]
    new_spec = lambda colblk: pl.BlockSpec((None, l, dm), lambda bi: (bi, 0, colblk))
    cache_spec = pl.BlockSpec((None, ncache, dm), lambda bi: (bi, 0, 0))
    return pl.pallas_call(
        _attn_sample_kernel,
        grid=(b,),
        in_specs=[new_spec(0), new_spec(1), new_spec(2), cache_spec, cache_spec,
                  _const_spec(bias_c.shape), _const_spec(bias_n.shape)],
        out_specs=pl.BlockSpec((None, l, dm), lambda bi: (bi, 0, 0)),
        out_shape=jax.ShapeDtypeStruct((b, l, dm), BF16),
        compiler_params=pltpu.CompilerParams(dimension_semantics=("parallel",),
                                             vmem_limit_bytes=VMEM_LIMIT),
        name="attn_sample",
    )(qkv, qkv, qkv, ck, cv, bias_c, bias_n)


def _prompt_bias(table, qb):
    nh = table.shape[0]
    rows, cols = qb, 3 * qb
    period = rows + cols
    far_past = jnp.broadcast_to(table[:, 2 * MAX_REL:], (nh, 3 * qb - 1 - MAX_REL))
    far_future = jnp.broadcast_to(table[:, :1], (nh, qb - 1 - MAX_REL))
    w = jnp.concatenate([far_past, table[:, ::-1], far_future, jnp.zeros((nh, 1), table.dtype)], axis=1)
    w = jnp.roll(w, -(rows - 1), axis=1)
    bias = jnp.tile(w, (1, rows))[:, :rows * (period - 1)].reshape(nh, rows, period - 1)[:, :, :cols]
    rq = np.arange(rows)[:, None]
    rk = np.arange(cols)[None, :]
    qc = rq // CHUNK
    kc = rk // CHUNK - (2 * qb) // CHUNK
    band = (kc <= qc) & (kc >= qc - C_BAND_CHUNKS)
    return jnp.where(jnp.asarray(band)[None], bias.astype(F32) * LOG2E, -jnp.inf)


def _sample_bias(table, l, ncache):
    qpos = PAST_LEN + np.arange(l)
    kpos = np.concatenate([PAST_LEN - ncache + np.arange(ncache), PAST_LEN + np.arange(l)])
    rel = np.clip(qpos[:, None] - kpos[None, :], -MAX_REL, MAX_REL) + MAX_REL
    qch = qpos // CHUNK
    kch = kpos // CHUNK
    valid = (kch[None, :] <= qch[:, None]) & (kch[None, :] >= qch[:, None] - C_BAND_CHUNKS)
    bias = jnp.where(jnp.asarray(valid)[None], table[:, rel].astype(F32), -jnp.inf)
    return bias[:, :, :ncache], bias[:, :, ncache:]


def _permute_w_in(w):
    d = w.shape[0]
    o_a = DN_QKV
    o_b = o_a + DN_HEADS
    o_gate = o_b + DN_HEADS
    o_gq = o_gate + HW
    o_gk = o_gq + GW
    o_gv = o_gk + GW
    o_lr = o_gv + HW
    o_gg = o_lr + GLA_RANK
    return jnp.concatenate(
        [w[:, :DN_QKV], w[:, o_gate:o_gq], w[:, o_gq:o_gk], w[:, o_gk:o_gv], w[:, o_gv:o_lr],
         w[:, o_gg:o_gg + HW], w[:, o_a:o_gate], w[:, o_lr:o_gg],
         jnp.zeros((d, LANE - 2 * DN_HEADS - GLA_RANK), w.dtype)], axis=1)


def _diag_blocks(s, nh, rows, cols):
    return jnp.stack([s[:, rows * h:rows * (h + 1), cols * h:cols * (h + 1)] for h in range(nh)], axis=1)


def _embed_blocks(s):
    b, nh, rows, cols = s.shape
    eye = jnp.eye(nh, dtype=s.dtype)
    return (s[:, :, :, None, :] * eye[None, :, None, :, None]).reshape(b, nh * rows, nh * cols)


def _trunk(x, conv0, dn0, gla0, ck, cv, w, prompt, tm):
    b, l, d = x.shape
    depth = w["ffn_norm"].shape[0]
    c = min(CHUNK, l)
    xf = x.reshape(b * l, d)
    convs, dns, glas, ks, vs = [], [], [], [], []
    for layer in range(depth):
        i = layer // 2
        if layer % 2 == 0:
            proj = _pre(xf, w["ab_norm"][i], w["ab_w_in"][i], tm).reshape(b, l, AB_PAD)
            conv0p = jnp.pad(conv0[i], ((0, 0), (SUBLANE - (CONV_W - 1), 0), (0, 0)))
            dn0bd = _embed_blocks(dn0[i])
            gl0bd = _embed_blocks(jnp.swapaxes(gla0[i], -1, -2))
            o, convn, dn_s, gl_s = _ab_mixer(proj, conv0p, dn0bd, gl0bd, w["dn_conv_w"][i], w["gparam"][i],
                                             w["gk_w2p"][i], w["gla_gk_b"][i], w["onorm"][i], c)
            convs.append(convn[:, SUBLANE - (CONV_W - 1):, :])
            dns.append(_diag_blocks(dn_s, DN_HEADS, DN_DK, DN_DV))
            glas.append(jnp.swapaxes(_diag_blocks(gl_s, GLA_HEADS, GLA_DV, GLA_DK), -1, -2))
            wo = w["ab_w_out"][i]
        else:
            qkv = _pre(xf, w["c_norm"][i], w["c_w_qkv"][i], tm).reshape(b, l, 3 * C_HEADS * C_HD)
            dm = C_HEADS * C_HD
            if prompt:
                o = _attn_prompt(qkv, _prompt_bias(w["c_rel_bias"][i], ATT_QB), ATT_QB)
                rows = min(C_PAST, l)
                ks.append(qkv[:, l - rows:, dm:2 * dm].reshape(b, rows, C_HEADS, C_HD))
                vs.append(qkv[:, l - rows:, 2 * dm:].reshape(b, rows, C_HEADS, C_HD))
            else:
                ncache = ck.shape[2]
                bias_c, bias_n = _sample_bias(w["c_rel_bias"][i], l, ncache)
                o = _attn_sample(qkv, ck[i].reshape(b, ncache, dm), cv[i].reshape(b, ncache, dm), bias_c, bias_n)
                ks.append(qkv[:, :, dm:2 * dm].reshape(b, l, C_HEADS, C_HD))
                vs.append(qkv[:, :, 2 * dm:].reshape(b, l, C_HEADS, C_HD))
            wo = w["c_w_out"][i]
        xf = _post(o.reshape(b * l, -1), xf, wo, w["ffn_norm"][layer], w["ffn_w_gu"][layer],
                   w["ffn_w_down"][layer], w["final_norm"], tm, final=(layer == depth - 1))
    return (xf.reshape(b, l, d), jnp.stack(convs), jnp.stack(dns), jnp.stack(glas), jnp.stack(ks), jnp.stack(vs))


def kernel(x_prompt, x_sample, state_dn_conv, state_dn, state_gla, cache_c_k, cache_c_v, ab_norm, ab_w_in, dn_conv_w, dn_a_log, dn_dt_bias, dn_out_norm, gla_gk_w2, gla_gk_b, gla_out_norm, ab_w_out, c_norm, c_w_qkv, c_rel_bias, c_w_out, ffn_norm, ffn_w_gu, ffn_w_down, final_norm):
    n_ab = ab_w_in.shape[0]
    bsz = x_prompt.shape[0]
    gparam = jnp.zeros((n_ab, SUBLANE, LANE), F32)
    gparam = gparam.at[:, 0, :DN_HEADS].set(dn_a_log).at[:, 1, :DN_HEADS].set(dn_dt_bias)
    gk_w2p = jnp.zeros((n_ab, LANE, GW), F32).at[:, 2 * DN_HEADS:2 * DN_HEADS + GLA_RANK, :].set(gla_gk_w2)
    onorm = jnp.stack([jnp.tile(dn_out_norm, (1, DN_HEADS)), jnp.tile(gla_out_norm, (1, GLA_HEADS))], axis=1)
    w = {
        "ab_norm": ab_norm, "c_norm": c_norm, "ffn_norm": ffn_norm, "final_norm": final_norm,
        "ab_w_in": jnp.stack([_permute_w_in(ab_w_in[i]) for i in range(n_ab)]).astype(BF16),
        "dn_conv_w": dn_conv_w, "gparam": gparam, "gk_w2p": gk_w2p.astype(BF16),
        "gla_gk_b": gla_gk_b[:, None, :], "onorm": onorm,
        "ab_w_out": ab_w_out.astype(BF16), "c_w_qkv": c_w_qkv.astype(BF16), "c_rel_bias": c_rel_bias,
        "c_w_out": c_w_out.astype(BF16), "ffn_w_gu": ffn_w_gu.astype(BF16), "ffn_w_down": ffn_w_down.astype(BF16),
    }
    conv0 = jnp.zeros((n_ab, bsz, CONV_W - 1, DN_QKV), F32)
    dn0 = jnp.zeros((n_ab, bsz, DN_HEADS, DN_DK, DN_DV), F32)
    gla0 = jnp.zeros((n_ab, bsz, GLA_HEADS, GLA_DK, GLA_DV), F32)
    y_p, conv_p, dn_p, gla_p, ck_p, cv_p = _trunk(x_prompt, conv0, dn0, gla0, None, None, w, True, 512)
    ts = x_sample.shape[0] * x_sample.shape[1]
    y_s, conv_s, dn_s, gla_s, ck_s, cv_s = _trunk(x_sample, state_dn_conv, state_dn, state_gla,
                                                  cache_c_k, cache_c_v, w, False, ts)
    return (y_p, y_s, conv_p, conv_s, dn_p, dn_s, gla_p, gla_s, ck_p, ck_s, cv_p, cv_s)
```

```python
import functools
import math

import numpy as np
import jax
import jax.numpy as jnp
from jax import lax
from jax.experimental import pallas as pl
from jax.experimental.pallas import tpu as pltpu

F32 = jnp.float32
BF16 = jnp.bfloat16

EPS = 1e-6
CHUNK = 64
PAST_LEN = 2048
D_MODEL = 1024
DN_HEADS = 8
DN_DK = 64
DN_DV = 64
CONV_W = 4
DN_QKV = DN_HEADS * (2 * DN_DK + DN_DV)
GLA_HEADS = 8
GLA_DK = 32
GLA_DV = 64
GLA_RANK = 16
GLA_NORMALIZER = 16.0
GLA_SUB = 16
C_HEADS = 16
C_HD = 64
C_BAND_CHUNKS = 8
C_PAST = C_BAND_CHUNKS * CHUNK
MAX_REL = 128
ATT_QB = 256
LOG2E = math.log2(math.e)
ATT_QSCALE = C_HD ** -0.5 * LOG2E

H = DN_HEADS
HW = DN_HEADS * DN_DV
GW = GLA_HEADS * GLA_DK
LANE = 128
SUBLANE = 8
NFG = HW // LANE
AB_STREAMS = 4

OFF_QKV = 0
OFF_DGATE = DN_QKV
OFF_GQ = OFF_DGATE + HW
OFF_GK = OFF_GQ + GW
OFF_GV = OFF_GK + GW
OFF_GGATE = OFF_GV + HW
OFF_SMALL = OFF_GGATE + HW
AB_PAD = OFF_SMALL + LANE

VMEM_LIMIT = 56 * 1024 * 1024


def _mm(a, b):
    return jnp.dot(a, b, preferred_element_type=F32)


def _mm_nt(a, b):
    return lax.dot_general(a, b, (((1,), (1,)), ((), ())), preferred_element_type=F32)


def _mm_tn(a, b):
    return lax.dot_general(a, b, (((0,), (0,)), ((), ())), preferred_element_type=F32)


def _split(x, n):
    parts = []
    r = x
    for t in range(n):
        p = r.astype(BF16)
        parts.append(p)
        if t + 1 < n:
            r = r - p.astype(F32)
    return parts


def _mmx(x, m, n):
    out = None
    for p in _split(x, n):
        t = _mm(p, m)
        out = t if out is None else out + t
    return out


def _mmx_left(m, x, n):
    out = None
    for p in _split(x, n):
        t = _mm(m, p)
        out = t if out is None else out + t
    return out


def _softplus(x):
    return jnp.maximum(x, 0.0) + jnp.log(1.0 + jnp.exp(-jnp.abs(x)))


def _tile_rows(x, n):
    return jnp.concatenate([x] * n, axis=0)


def _grp(x, g):
    return x[:, LANE * g:LANE * (g + 1)]


def _rms(x, g):
    ms = jnp.mean(x * x, axis=-1, keepdims=True)
    return (x * lax.rsqrt(ms + EPS)) * g


def _const_spec(shape):
    nd = len(shape)
    return pl.BlockSpec(shape, lambda *_: (0,) * nd, pipeline_mode=pl.Buffered(1))


def _pre_kernel(x_ref, g_ref, w_ref, o_ref, *, col_chunk):
    xn = _rms(x_ref[...], g_ref[...]).astype(BF16)
    n = o_ref.shape[-1]
    for c0 in range(0, n, col_chunk):
        c1 = min(n, c0 + col_chunk)
        o_ref[:, c0:c1] = _mm(xn, w_ref[:, c0:c1])


def _pre(x, g, w, tm):
    t, d = x.shape
    n = w.shape[1]
    return pl.pallas_call(
        functools.partial(_pre_kernel, col_chunk=512),
        grid=(t // tm,),
        in_specs=[pl.BlockSpec((tm, d), lambda i: (i, 0)),
                  _const_spec((1, d)),
                  _const_spec((d, n))],
        out_specs=pl.BlockSpec((tm, n), lambda i: (i, 0)),
        out_shape=jax.ShapeDtypeStruct((t, n), F32),
        compiler_params=pltpu.CompilerParams(dimension_semantics=("parallel",),
                                             vmem_limit_bytes=VMEM_LIMIT),
        name="pre_proj",
    )(x, g.reshape(1, d), w)


def _post_kernel(o_ref, x_ref, wo_ref, g_ref, wgu_ref, wd_ref, gf_ref, out_ref, acc_ref, *, ff_chunk, final):
    x1 = x_ref[...] + _mm(o_ref[...], wo_ref[...])
    xn = _rms(x1, g_ref[...]).astype(BF16)
    dff = wd_ref.shape[0]
    for f0 in range(0, dff, ff_chunk):
        gate = _mm(xn, wgu_ref[:, f0:f0 + ff_chunk])
        up = _mm(xn, wgu_ref[:, dff + f0:dff + f0 + ff_chunk])
        a = (gate * jax.nn.sigmoid(gate) * up).astype(BF16)
        part = _mm(a, wd_ref[f0:f0 + ff_chunk, :])
        if f0 == 0:
            acc_ref[...] = part
        else:
            acc_ref[...] += part
    x2 = x1 + acc_ref[...]
    if final:
        x2 = _rms(x2, gf_ref[...])
    out_ref[...] = x2


def _post(o, x, wo, g, wgu, wd, gf, tm, final):
    t, d = x.shape
    dm = o.shape[1]
    dff = wd.shape[0]
    return pl.pallas_call(
        functools.partial(_post_kernel, ff_chunk=256, final=final),
        grid=(t // tm,),
        in_specs=[pl.BlockSpec((tm, dm), lambda i: (i, 0)),
                  pl.BlockSpec((tm, d), lambda i: (i, 0)),
                  _const_spec((dm, d)),
                  _const_spec((1, d)),
                  _const_spec((d, 2 * dff)),
                  _const_spec((dff, d)),
                  _const_spec((1, d))],
        out_specs=pl.BlockSpec((tm, d), lambda i: (i, 0)),
        out_shape=jax.ShapeDtypeStruct((t, d), F32),
        scratch_shapes=[pltpu.VMEM((tm, d), F32)],
        compiler_params=pltpu.CompilerParams(dimension_semantics=("parallel",),
                                             vmem_limit_bytes=VMEM_LIMIT),
        name="post_ffn",
    )(o, x, wo, g.reshape(1, d), wgu, wd, gf.reshape(1, d))


def _score_group_of_fg(fg, c):
    return (2 * fg * c) // LANE


@functools.lru_cache(maxsize=None)
def _gla_pairs(c):
    hs = LANE // c
    pairs = []
    for sg in range(H * c // LANE):
        for qd in range(GW // LANE):
            if set(range(sg * hs, (sg + 1) * hs)) & set(range(4 * qd, 4 * qd + 4)):
                pairs.append((sg, qd))
    return tuple(pairs)


@functools.lru_cache(maxsize=None)
def _ab_consts(c):
    w = H * c
    hs = LANE // c
    nsg = w // LANE
    nsb = c // GLA_SUB
    i = np.arange(c)
    r = np.arange(LANE)
    w_pos = np.arange(w) % c
    ltri = (i[None, :] <= i[:, None])
    eye_w = (i[:, None] == w_pos[None, :])
    causal_w = (w_pos[None, :] <= i[:, None])
    strict_w = (w_pos[None, :] < i[:, None])
    dmask = causal_w & ((i[:, None] // GLA_SUB) == (w_pos[None, :] // GLA_SUB))
    row_hl = r // c
    row_j = r % c
    m_ss = (row_hl[:, None] == row_hl[None, :])
    m_ff = ((r // DN_DV)[:, None] == (r // DN_DV)[None, :])
    m_sf = np.stack([(_score_group_of_fg(fg, c) * hs + row_hl)[:, None] == (2 * fg + r // DN_DV)[None, :]
                     for fg in range(NFG)])
    mg = np.stack([(sg * hs + row_hl)[:, None] == (r // 16)[None, :] for sg in range(nsg)])
    pairs = _gla_pairs(c)
    m_sq = np.stack([(sg * hs + row_hl)[:, None] == (4 * qd + r // GLA_DK)[None, :] for sg, qd in pairs])
    m_sq3 = np.stack([np.concatenate([m & ((row_j // GLA_SUB) == j)[:, None] for j in range(max(nsb - 1, 1))],
                                     axis=1) for m in m_sq])
    m_fq = np.stack([(2 * fg + r // DN_DV)[:, None] == (4 * (fg // 2) + r // GLA_DK)[None, :]
                     for fg in range(NFG)])
    sel = np.zeros((6, LANE, LANE), np.float32)
    cst = np.zeros((SUBLANE, LANE), np.float32)
    for h in range(H):
        for p in range(3):
            sel[p, h, 16 * h + p] = 1.0
            sel[3 + p, h, 16 * h + 3 + p] = -1.0
            cst[0, 16 * h + 3 + p] = 1.0
            cst[1, 16 * h + p] = 1.0
    expg = np.zeros((LANE, HW), np.float32)
    expb = np.zeros((LANE, HW), np.float32)
    for h in range(H):
        expg[h, DN_DV * h:DN_DV * (h + 1)] = 1.0
        expb[H + h, DN_DV * h:DN_DV * (h + 1)] = 1.0
    f = lambda a: jnp.asarray(a, F32)
    b = lambda a: jnp.asarray(a, BF16)
    return dict(ltri=b(ltri), eye_w=f(eye_w), causal_w=f(causal_w), strict_w=f(strict_w), dmask=f(dmask),
                m_ss=b(m_ss), m_sf=b(m_sf), m_ffb=b(m_ff), m_ff=f(m_ff), mg=b(mg), m_sq=b(m_sq), m_sq3=b(m_sq3),
                m_fq=f(m_fq), sel=b(sel), cst=f(cst), expg=b(expg), expb=b(expb))


_AB_CONST_ORDER = ("ltri", "eye_w", "causal_w", "strict_w", "dmask", "m_ss", "m_sf", "m_ffb", "m_ff", "mg",
                   "m_sq", "m_sq3", "m_fq", "sel", "cst", "expg", "expb")


def _headsum(xs, bones):
    rows = xs[0].shape[0]
    pieces = [_grp(p, g) for x in xs for p in _split(x, 2) for g in range(NFG)]
    y = _mm(jnp.concatenate(pieces, axis=0), bones)
    outs = []
    for n in range(len(xs)):
        base = 2 * NFG * n
        outs.append(jnp.concatenate(
            [y[(base + g) * rows:(base + g + 1) * rows] + y[(base + NFG + g) * rows:(base + NFG + g + 1) * rows]
             for g in range(NFG)], axis=1))
    return outs


def _ab_kernel(proj_ref, conv0_ref, dn0_ref, gl0_ref, convw_ref, gparam_ref, w2_ref, gkb_ref, onorm_ref,
               ltri_ref, eye_ref, causal_ref, strict_ref, dmask_ref, mss_ref, msf_ref, mffb_ref, mff_ref, mg_ref,
               msq_ref, msq3_ref, mfq_ref, sel_ref, cst_ref, expg_ref, expb_ref,
               o_ref, convn_ref, dns_ref, gls_ref,
               xp_ref, sdn_ref, sgl_ref, *, c, nc, ns):
    i = pl.program_id(1)
    nsb = c // GLA_SUB
    hs = LANE // c
    nsg = H * c // LANE
    sg_of = [_score_group_of_fg(fg, c) for fg in range(NFG)]
    streams = range(ns)

    @pl.when(i == 0)
    def _load_state():
        xp_ref[:, 0:SUBLANE, :] = conv0_ref[...]
        sdn_ref[...] = dn0_ref[...]
        sgl_ref[...] = gl0_ref[...]

    cw = convw_ref[...]
    bones = mffb_ref[...]
    mss = mss_ref[...]
    ltri = ltri_ref[...]
    gp = gparam_ref[...]
    onorm = onorm_ref[...]

    q, k, v = [], [], []
    for s in streams:
        x = proj_ref[s, :, OFF_QKV:OFF_QKV + DN_QKV]
        xp_ref[s, SUBLANE:SUBLANE + c, :] = x
        xa = xp_ref[s]
        y = pltpu.roll(xa, c + 3, axis=0)[0:c] * cw[0:1, :]
        y = y + pltpu.roll(xa, c + 2, axis=0)[0:c] * cw[1:2, :]
        y = y + pltpu.roll(xa, c + 1, axis=0)[0:c] * cw[2:3, :]
        y = y + x * cw[3:4, :]
        xp_ref[s, 0:SUBLANE, :] = x[c - SUBLANE:c, :]
        qkv = y * jax.nn.sigmoid(y)
        q.append(qkv[:, 0:HW])
        k.append(qkv[:, HW:2 * HW])
        v.append(qkv[:, 2 * HW:3 * HW])

    for s in streams:
        sq, sk = _headsum([q[s] * q[s], k[s] * k[s]], bones)
        q[s] = q[s] * lax.rsqrt(sq + EPS) * (DN_DK ** -0.5)
        k[s] = k[s] * lax.rsqrt(sk + EPS)

    pairs = _gla_pairs(c)
    small = [proj_ref[s, :, OFF_SMALL:OFF_SMALL + LANE] for s in streams]
    gl = {}

    def gla_gate_logits():
        gl["z"] = [_mm(small[s].astype(BF16), w2_ref[...]) + gkb_ref[...] for s in streams]

    def gla_cumsum():
        gl["bcum"] = [_mmx_left(ltri, -_softplus(-gl["z"][s]) * (1.0 / GLA_NORMALIZER), 3) for s in streams]

    def gla_scores_and_state():
        gl["a"], gl["o_inter"], gl["v2b"] = [], [], []
        for s in streams:
            bcum = gl["bcum"][s]
            q2 = proj_ref[s, :, OFF_GQ:OFF_GQ + GW] * (GLA_DK ** -0.5)
            k2 = proj_ref[s, :, OFF_GK:OFF_GK + GW]
            v2b = proj_ref[s, :, OFF_GV:OFF_GV + HW].astype(BF16)
            blast = bcum[c - 1:c, :]
            qe = (q2 * jnp.exp(bcum)).astype(BF16)
            kdec = (k2 * jnp.exp(blast - bcum)).astype(BF16)
            eblast = jnp.exp(blast)
            rmid = jnp.concatenate(
                [jnp.broadcast_to(bcum[GLA_SUB * t + GLA_SUB // 2:GLA_SUB * t + GLA_SUB // 2 + 1, :], (GLA_SUB, GW))
                 for t in range(nsb)], axis=0)
            qm = (q2 * jnp.exp(bcum - rmid)).astype(BF16)
            km = (k2 * jnp.exp(rmid - bcum)).astype(BF16)
            parts = [None] * nsg
            for n, (sg, qd) in enumerate(pairs):
                t = _mm_nt(_grp(qm, qd), _tile_rows(_grp(km, qd), hs) * msq_ref[n])
                parts[sg] = t if parts[sg] is None else parts[sg] + t
            a_gl = dmask_ref[...] * jnp.concatenate(parts, axis=1)
            if nsb > 1:
                rend = jnp.concatenate(
                    [jnp.broadcast_to(bcum[GLA_SUB * (t + 1) - 1:GLA_SUB * (t + 1), :], (GLA_SUB, GW))
                     for t in range(nsb)], axis=0)
                kr = (k2 * jnp.exp(rend - bcum)).astype(BF16)
                rowi = lax.broadcasted_iota(jnp.int32, (c, GW), 0)
                qs = []
                for t in range(nsb - 1):
                    e = jnp.where(rowi >= GLA_SUB * (t + 1), bcum - rend[GLA_SUB * t:GLA_SUB * t + 1, :], -jnp.inf)
                    qs.append((q2 * jnp.exp(e)).astype(BF16))
                parts = [None] * nsg
                for n, (sg, qd) in enumerate(pairs):
                    qcat = jnp.concatenate([_grp(x, qd) for x in qs], axis=1)
                    kcat = _tile_rows(jnp.concatenate([_grp(kr, qd)] * (nsb - 1), axis=1), hs) * msq3_ref[n]
                    t = _mm_nt(qcat, kcat)
                    parts[sg] = t if parts[sg] is None else parts[sg] + t
                a_gl = a_gl + jnp.concatenate(parts, axis=1)
            o_inter = []
            for fg in range(NFG):
                qd = fg // 2
                st = sgl_ref[s, fg]
                o_inter.append(_mm_nt(_grp(qe, qd), st.astype(BF16)))
                sgl_ref[s, fg] = st * _grp(eblast, qd) + _mm_tn(_grp(v2b, fg), _grp(kdec, qd)) * mfq_ref[fg]
            gl["a"].append(a_gl.astype(BF16))
            gl["o_inter"].append(o_inter)
            gl["v2b"].append(v2b)

    def gla_out():
        gl["o"] = [jnp.concatenate(
            [gl["o_inter"][s][fg]
             + _mm(_grp(gl["a"][s], sg_of[fg]), _tile_rows(_grp(gl["v2b"][s], fg), hs) * msf_ref[fg])
             for fg in range(NFG)], axis=1) for s in streams]

    gla_gate_logits()
    gcum_parts = []
    for s in streams:
        g_full = -jnp.exp(gp[0:1, :]) * _softplus(small[s] + gp[1:2, :])
        gcum_parts.append(_split(_mmx_left(ltri, g_full, 3), 3))
    gla_cumsum()
    beta_w, gcum_w, g1, g2 = [], [], [], []
    for s in streams:
        gw, a1, a2 = None, cst_ref[0:1, :], cst_ref[1:2, :]
        for p in range(3):
            t = _mm(gcum_parts[s][p], expg_ref[...])
            gw = t if gw is None else gw + t
            a1 = a1 + _mm(gcum_parts[s][p], sel_ref[p])
            a2 = a2 + _mm(gcum_parts[s][p], sel_ref[3 + p])
        gcum_w.append(gw)
        g1.append(a1.astype(BF16))
        g2.append(a2.astype(BF16))
        beta_w.append(_mmx(jax.nn.sigmoid(small[s]), expb_ref[...], 2))
    eg_w = [jnp.exp(gcum_w[s]) for s in streams]
    edec_w = [jnp.exp(gcum_w[s][c - 1:c, :] - gcum_w[s]) for s in streams]
    eglast_w = [jnp.exp(gcum_w[s][c - 1:c, :]) for s in streams]
    gla_scores_and_state()

    decay_w = []
    for s in streams:
        d_w = jnp.concatenate([_mm_nt(g1[s], _tile_rows(g2[s], hs) * mg_ref[sg]) for sg in range(nsg)], axis=1)
        decay_w.append(jnp.exp(jnp.where(causal_ref[...] > 0.0, d_w, -jnp.inf)))

    kb = [k[s] * beta_w[s] for s in streams]
    a_w, qkd_b = [], []
    for s in streams:
        kq = jnp.concatenate([kb[s], q[s]], axis=0).astype(BF16)
        kbf = k[s].astype(BF16)
        parts = [None] * nsg
        for fg in range(NFG):
            t = _mm_nt(_grp(kq, fg), _tile_rows(_grp(kbf, fg), hs) * msf_ref[fg])
            parts[sg_of[fg]] = t if parts[sg_of[fg]] is None else parts[sg_of[fg]] + t
        r = jnp.concatenate(parts, axis=1)
        a_w.append(strict_ref[...] * r[0:c] * decay_w[s])
        qkd_b.append((r[c:2 * c] * decay_w[s]).astype(BF16))

    def ssprod(x_w, y_w):
        xb = x_w.astype(BF16)
        yb = y_w.astype(BF16)
        return jnp.concatenate([_mm(_grp(xb, sg), _tile_rows(_grp(yb, sg), hs) * mss) for sg in range(nsg)], axis=1)

    nlev = int(math.log2(c))
    pw = [-a_w[s] for s in streams]
    t_w = [eye_ref[...] + pw[s] for s in streams]
    pw = [ssprod(pw[s], pw[s]) for s in streams]
    gla_out()
    for _ in range(2, nlev):
        rr = [ssprod(jnp.concatenate([pw[s], t_w[s]], axis=0), pw[s]) for s in streams]
        t_w = [t_w[s] + rr[s][c:2 * c] for s in streams]
        pw = [rr[s][0:c] for s in streams]
    rr = [ssprod(t_w[s], pw[s]) for s in streams]
    tb = [(t_w[s] + rr[s]).astype(BF16) for s in streams]

    vbb = [(v[s] * beta_w[s]).astype(BF16) for s in streams]
    kbe = [(kb[s] * eg_w[s]).astype(BF16) for s in streams]
    qeg = [q[s] * eg_w[s] for s in streams]
    kdc = [(k[s] * edec_w[s]).astype(BF16) for s in streams]
    sf_pairs = [(s, fg) for fg in range(NFG) for s in streams]
    uw, rs, s_dn = {}, {}, {}
    for s, fg in sf_pairs:
        msf = msf_ref[fg]
        rhs = jnp.concatenate([_tile_rows(_grp(vbb[s], fg), hs) * msf, _tile_rows(_grp(kbe[s], fg), hs) * msf], axis=1)
        uw[s, fg] = _mm(_grp(tb[s], sg_of[fg]), rhs)
    for s, fg in sf_pairs:
        s_dn[s, fg] = sdn_ref[s, fg]
        lhs = jnp.concatenate([uw[s, fg][:, LANE:2 * LANE], _grp(qeg[s], fg)], axis=0).astype(BF16)
        rs[s, fg] = _mm(lhs, s_dn[s, fg].astype(BF16))
    o_dn = [[None] * NFG for _ in streams]
    for s, fg in sf_pairs:
        vnb = (uw[s, fg][:, 0:LANE] - rs[s, fg][0:c]).astype(BF16)
        o_dn[s][fg] = rs[s, fg][c:2 * c] + _mm(_grp(qkd_b[s], sg_of[fg]), _tile_rows(vnb, hs) * msf_ref[fg])
        sdn_ref[s, fg] = s_dn[s, fg] * _grp(eglast_w[s], fg) + _mm_tn(_grp(kdc[s], fg), vnb) * mff_ref[...]
    o_dn = [jnp.concatenate(o_dn[s], axis=1) for s in streams]
    o_gl = gl["o"]

    for s in streams:
        md, mgl = _headsum([o_dn[s] * o_dn[s], o_gl[s] * o_gl[s]], bones)
        gd = proj_ref[s, :, OFF_DGATE:OFF_DGATE + HW]
        gg = proj_ref[s, :, OFF_GGATE:OFF_GGATE + HW]
        y_dn = (o_dn[s] * lax.rsqrt(md * (1.0 / DN_DV) + EPS)) * onorm[0:1, :] * (gd * jax.nn.sigmoid(gd))
        y_gl = (o_gl[s] * lax.rsqrt(mgl * (1.0 / GLA_DV) + EPS)) * onorm[1:2, :] * (gg * jax.nn.sigmoid(gg))
        o_ref[s, :, 0:HW] = y_dn.astype(BF16)
        o_ref[s, :, HW:2 * HW] = y_gl.astype(BF16)

    @pl.when(i == nc - 1)
    def _store_state():
        convn_ref[...] = xp_ref[:, 0:SUBLANE, :]
        dns_ref[...] = sdn_ref[...]
        gls_ref[...] = sgl_ref[...]


def _ab_mixer(proj, conv0, dn0, gl0, convw, gparam, w2p, gkb, onorm, c):
    b, l, _ = proj.shape
    nc = l // c
    ns = AB_STREAMS
    consts = _ab_consts(c)
    cvals = [consts[n] for n in _AB_CONST_ORDER]
    per_b = lambda shape: pl.BlockSpec((ns,) + shape, lambda bi, ci: (bi,) + (0,) * len(shape))
    state = (NFG, LANE, LANE)
    in_specs = ([pl.BlockSpec((ns, c, AB_PAD), lambda bi, ci: (bi, ci, 0)),
                 per_b((SUBLANE, DN_QKV)), per_b(state), per_b(state),
                 _const_spec(convw.shape), _const_spec(gparam.shape), _const_spec(w2p.shape),
                 _const_spec(gkb.shape), _const_spec(onorm.shape)]
                + [_const_spec(a.shape) for a in cvals])
    out_specs = [pl.BlockSpec((ns, c, 2 * HW), lambda bi, ci: (bi, ci, 0)),
                 per_b((SUBLANE, DN_QKV)), per_b(state), per_b(state)]
    out_shape = [jax.ShapeDtypeStruct((b, l, 2 * HW), BF16),
                 jax.ShapeDtypeStruct((b, SUBLANE, DN_QKV), F32),
                 jax.ShapeDtypeStruct((b,) + state, F32),
                 jax.ShapeDtypeStruct((b,) + state, F32)]
    return pl.pallas_call(
        functools.partial(_ab_kernel, c=c, nc=nc, ns=ns),
        grid=(b // ns, nc),
        in_specs=in_specs,
        out_specs=out_specs,
        out_shape=out_shape,
        scratch_shapes=[pltpu.VMEM((ns, SUBLANE + c, DN_QKV), F32),
                        pltpu.VMEM((ns,) + state, F32),
                        pltpu.VMEM((ns,) + state, F32)],
        compiler_params=pltpu.CompilerParams(dimension_semantics=("parallel", "arbitrary"),
                                             vmem_limit_bytes=VMEM_LIMIT),
        name="ab_mixer",
    )(proj, conv0, dn0, gl0, convw, gparam, w2p, gkb, onorm, *cvals)


def _dn_state_to_groups(s):
    b = s.shape[0]
    out = jnp.zeros((b, NFG, LANE, LANE), s.dtype)
    for h in range(DN_HEADS):
        hl = h % 2
        out = out.at[:, h // 2, DN_DK * hl:DN_DK * (hl + 1), DN_DV * hl:DN_DV * (hl + 1)].set(s[:, h])
    return out


def _dn_state_from_groups(g):
    return jnp.stack([g[:, h // 2, DN_DK * (h % 2):DN_DK * (h % 2 + 1), DN_DV * (h % 2):DN_DV * (h % 2 + 1)]
                      for h in range(DN_HEADS)], axis=1)


def _gla_state_to_groups(s):
    b = s.shape[0]
    out = jnp.zeros((b, NFG, LANE, LANE), s.dtype)
    for h in range(GLA_HEADS):
        hl, hq = h % 2, h % 4
        out = out.at[:, h // 2, GLA_DV * hl:GLA_DV * (hl + 1), GLA_DK * hq:GLA_DK * (hq + 1)].set(
            jnp.swapaxes(s[:, h], -1, -2))
    return out


def _gla_state_from_groups(g):
    return jnp.stack([jnp.swapaxes(g[:, h // 2, GLA_DV * (h % 2):GLA_DV * (h % 2 + 1),
                                     GLA_DK * (h % 4):GLA_DK * (h % 4 + 1)], -1, -2)
                      for h in range(GLA_HEADS)], axis=1)


def _softmax_rows(parts):
    m = None
    for s in parts:
        t = jnp.max(s, axis=-1, keepdims=True)
        m = t if m is None else jnp.maximum(m, t)
    es = [jnp.exp(s - m) for s in parts]
    l = None
    for e in es:
        t = jnp.sum(e, axis=-1, keepdims=True)
        l = t if l is None else l + t
    inv = 1.0 / l
    return [e * inv for e in es]


def _fold_lanes(x, op):
    out = x[:, 0:LANE]
    for c0 in range(LANE, x.shape[1], LANE):
        out = op(out, x[:, c0:c0 + LANE])
    return out


def _attn_prompt_kernel(q_ref, k0_ref, k1_ref, k2_ref, v0_ref, v1_ref, v2_ref, bias_ref, o_ref, *, qb):
    i = pl.program_id(1)
    offs = (jnp.where(i >= 2, 0.0, -jnp.inf), jnp.where(i >= 1, 0.0, -jnp.inf), 0.0)
    k_refs = (k0_ref, k1_ref, k2_ref)
    v_refs = (v0_ref, v1_ref, v2_ref)
    for h in range(C_HEADS):
        sl = slice(C_HD * h, C_HD * (h + 1))
        qh = (q_ref[:, sl] * ATT_QSCALE).astype(BF16)
        ss, em = [], None
        for t in range(3):
            s = _mm_nt(qh, k_refs[t][:, sl].astype(BF16)) + bias_ref[h, :, t * qb:(t + 1) * qb]
            et = _fold_lanes(s, jnp.maximum) + offs[t]
            em = et if em is None else jnp.maximum(em, et)
            ss.append(s)
        m = jnp.max(em, axis=-1, keepdims=True)
        acc, el = None, None
        for t in range(3):
            p = jnp.exp2(ss[t] - (m - offs[t]))
            lt = _fold_lanes(p, jnp.add)
            at = _mm(p.astype(BF16), v_refs[t][:, sl].astype(BF16))
            el = lt if el is None else el + lt
            acc = at if acc is None else acc + at
        l = jnp.sum(el, axis=-1, keepdims=True)
        o_ref[:, sl] = (acc * (1.0 / l)).astype(BF16)


def _attn_prompt(qkv, bias, qb):
    b, l, _ = qkv.shape
    dm = C_HEADS * C_HD
    kv_spec = lambda colblk, back: pl.BlockSpec(
        (None, qb, dm), lambda bi, i: (bi, jnp.maximum(i - back, 0), colblk))
    return pl.pallas_call(
        functools.partial(_attn_prompt_kernel, qb=qb),
        grid=(b, l // qb),
        in_specs=[pl.BlockSpec((None, qb, dm), lambda bi, i: (bi, i, 0)),
                  kv_spec(1, 2), kv_spec(1, 1), kv_spec(1, 0),
                  kv_spec(2, 2), kv_spec(2, 1), kv_spec(2, 0),
                  _const_spec(bias.shape)],
        out_specs=pl.BlockSpec((None, qb, dm), lambda bi, i: (bi, i, 0)),
        out_shape=jax.ShapeDtypeStruct((b, l, dm), BF16),
        compiler_params=pltpu.CompilerParams(dimension_semantics=("parallel", "parallel"),
                                             vmem_limit_bytes=VMEM_LIMIT),
        name="attn_prompt",
    )(qkv, qkv, qkv, qkv, qkv, qkv, qkv, bias)


def _attn_sample_kernel(q_ref, k_ref, v_ref, ck_ref, cv_ref, bc_ref, bn_ref, o_ref):
    scale = C_HD ** -0.5
    for h in range(C_HEADS):
        sl = slice(C_HD * h, C_HD * (h + 1))
        qh = q_ref[:, sl].astype(BF16)
        sc = _mm_nt(qh, ck_ref[:, sl].astype(BF16)) * scale + bc_ref[h]
        sn = _mm_nt(qh, k_ref[:, sl].astype(BF16)) * scale + bn_ref[h]
        pc, pn = _softmax_rows([sc, sn])
        oh = _mm(pc.astype(BF16), cv_ref[:, sl].astype(BF16)) + _mm(pn.astype(BF16), v_ref[:, sl].astype(BF16))
        o_ref[:, sl] = oh.astype(BF16)


def _attn_sample(qkv, ck, cv, bias_c, bias_n):
    b, l, _ = qkv.shape
    dm = C_HEADS * C_HD
    ncache = ck.shape[1]
    new_spec = lambda colblk: pl.BlockSpec((None, l, dm), lambda bi: (bi, 0, colblk))
    cache_spec = pl.BlockSpec((None, ncache, dm), lambda bi: (bi, 0, 0))
    return pl.pallas_call(
        _attn_sample_kernel,
        grid=(b,),
        in_specs=[new_spec(0), new_spec(1), new_spec(2), cache_spec, cache_spec,
                  _const_spec(bias_c.shape), _const_spec(bias_n.shape)],
        out_specs=pl.BlockSpec((None, l, dm), lambda bi: (bi, 0, 0)),
        out_shape=jax.ShapeDtypeStruct((b, l, dm), BF16),
        compiler_params=pltpu.CompilerParams(dimension_semantics=("parallel",),
                                             vmem_limit_bytes=VMEM_LIMIT),
        name="attn_sample",
    )(qkv, qkv, qkv, ck, cv, bias_c, bias_n)


def _prompt_bias(table, qb):
    nh = table.shape[0]
    rows, cols = qb, 3 * qb
    period = rows + cols
    far_past = jnp.broadcast_to(table[:, 2 * MAX_REL:], (nh, 3 * qb - 1 - MAX_REL))
    far_future = jnp.broadcast_to(table[:, :1], (nh, qb - 1 - MAX_REL))
    w = jnp.concatenate([far_past, table[:, ::-1], far_future, jnp.zeros((nh, 1), table.dtype)], axis=1)
    w = jnp.roll(w, -(rows - 1), axis=1)
    bias = jnp.tile(w, (1, rows))[:, :rows * (period - 1)].reshape(nh, rows, period - 1)[:, :, :cols]
    rq = np.arange(rows)[:, None]
    rk = np.arange(cols)[None, :]
    qc = rq // CHUNK
    kc = rk // CHUNK - (2 * qb) // CHUNK
    band = (kc <= qc) & (kc >= qc - C_BAND_CHUNKS)
    return jnp.where(jnp.asarray(band)[None], bias.astype(F32) * LOG2E, -jnp.inf)


def _sample_bias(table, l, ncache):
    qpos = PAST_LEN + np.arange(l)
    kpos = np.concatenate([PAST_LEN - ncache + np.arange(ncache), PAST_LEN + np.arange(l)])
    rel = np.clip(qpos[:, None] - kpos[None, :], -MAX_REL, MAX_REL) + MAX_REL
    qch = qpos // CHUNK
    kch = kpos // CHUNK
    valid = (kch[None, :] <= qch[:, None]) & (kch[None, :] >= qch[:, None] - C_BAND_CHUNKS)
    bias = jnp.where(jnp.asarray(valid)[None], table[:, rel].astype(F32), -jnp.inf)
    return bias[:, :, :ncache], bias[:, :, ncache:]


def _permute_w_in(w):
    d = w.shape[0]
    o_a = DN_QKV
    o_b = o_a + DN_HEADS
    o_gate = o_b + DN_HEADS
    o_gq = o_gate + HW
    o_gk = o_gq + GW
    o_gv = o_gk + GW
    o_lr = o_gv + HW
    o_gg = o_lr + GLA_RANK
    return jnp.concatenate(
        [w[:, :DN_QKV], w[:, o_gate:o_gq], w[:, o_gq:o_gk], w[:, o_gk:o_gv], w[:, o_gv:o_lr],
         w[:, o_gg:o_gg + HW], w[:, o_a:o_gate], w[:, o_lr:o_gg],
         jnp.zeros((d, LANE - 2 * DN_HEADS - GLA_RANK), w.dtype)], axis=1)


def _trunk(x, conv0, dn0, gla0, ck, cv, w, prompt, tm):
    b, l, d = x.shape
    depth = w["ffn_norm"].shape[0]
    c = min(CHUNK, l)
    xf = x.reshape(b * l, d)
    convs, dns, glas, ks, vs = [], [], [], [], []
    for layer in range(depth):
        i = layer // 2
        if layer % 2 == 0:
            proj = _pre(xf, w["ab_norm"][i], w["ab_w_in"][i], tm).reshape(b, l, AB_PAD)
            conv0p = jnp.pad(conv0[i], ((0, 0), (SUBLANE - (CONV_W - 1), 0), (0, 0)))
            o, convn, dn_s, gl_s = _ab_mixer(proj, conv0p, _dn_state_to_groups(dn0[i]),
                                             _gla_state_to_groups(gla0[i]), w["dn_conv_w"][i], w["gparam"][i],
                                             w["gk_w2p"][i], w["gla_gk_b"][i], w["onorm"][i], c)
            convs.append(convn[:, SUBLANE - (CONV_W - 1):, :])
            dns.append(_dn_state_from_groups(dn_s))
            glas.append(_gla_state_from_groups(gl_s))
            wo = w["ab_w_out"][i]
        else:
            qkv = _pre(xf, w["c_norm"][i], w["c_w_qkv"][i], tm).reshape(b, l, 3 * C_HEADS * C_HD)
            dm = C_HEADS * C_HD
            if prompt:
                o = _attn_prompt(qkv, _prompt_bias(w["c_rel_bias"][i], ATT_QB), ATT_QB)
                rows = min(C_PAST, l)
                ks.append(qkv[:, l - rows:, dm:2 * dm].reshape(b, rows, C_HEADS, C_HD))
                vs.append(qkv[:, l - rows:, 2 * dm:].reshape(b, rows, C_HEADS, C_HD))
            else:
                ncache = ck.shape[2]
                bias_c, bias_n = _sample_bias(w["c_rel_bias"][i], l, ncache)
                o = _attn_sample(qkv, ck[i].reshape(b, ncache, dm), cv[i].reshape(b, ncache, dm), bias_c, bias_n)
                ks.append(qkv[:, :, dm:2 * dm].reshape(b, l, C_HEADS, C_HD))
                vs.append(qkv[:, :, 2 * dm:].reshape(b, l, C_HEADS, C_HD))
            wo = w["c_w_out"][i]
        xf = _post(o.reshape(b * l, -1), xf, wo, w["ffn_norm"][layer], w["ffn_w_gu"][layer],
                   w["ffn_w_down"][layer], w["final_norm"], tm, final=(layer == depth - 1))
    return (xf.reshape(b, l, d), jnp.stack(convs), jnp.stack(dns), jnp.stack(glas), jnp.stack(ks), jnp.stack(vs))


def kernel(x_prompt, x_sample, state_dn_conv, state_dn, state_gla, cache_c_k, cache_c_v, ab_norm, ab_w_in, dn_conv_w, dn_a_log, dn_dt_bias, dn_out_norm, gla_gk_w2, gla_gk_b, gla_out_norm, ab_w_out, c_norm, c_w_qkv, c_rel_bias, c_w_out, ffn_norm, ffn_w_gu, ffn_w_down, final_norm):
    n_ab = ab_w_in.shape[0]
    bsz = x_prompt.shape[0]
    gparam = jnp.zeros((n_ab, SUBLANE, LANE), F32)
    gparam = gparam.at[:, 0, :DN_HEADS].set(dn_a_log).at[:, 1, :DN_HEADS].set(dn_dt_bias)
    gk_w2p = jnp.zeros((n_ab, LANE, GW), F32).at[:, 2 * DN_HEADS:2 * DN_HEADS + GLA_RANK, :].set(gla_gk_w2)
    onorm = jnp.stack([jnp.tile(dn_out_norm, (1, DN_HEADS)), jnp.tile(gla_out_norm, (1, GLA_HEADS))], axis=1)
    w = {
        "ab_norm": ab_norm, "c_norm": c_norm, "ffn_norm": ffn_norm, "final_norm": final_norm,
        "ab_w_in": jnp.stack([_permute_w_in(ab_w_in[i]) for i in range(n_ab)]).astype(BF16),
        "dn_conv_w": dn_conv_w, "gparam": gparam, "gk_w2p": gk_w2p.astype(BF16),
        "gla_gk_b": gla_gk_b[:, None, :], "onorm": onorm,
        "ab_w_out": ab_w_out.astype(BF16), "c_w_qkv": c_w_qkv.astype(BF16), "c_rel_bias": c_rel_bias,
        "c_w_out": c_w_out.astype(BF16), "ffn_w_gu": ffn_w_gu.astype(BF16), "ffn_w_down": ffn_w_down.astype(BF16),
    }
    conv0 = jnp.zeros((n_ab, bsz, CONV_W - 1, DN_QKV), F32)
    dn0 = jnp.zeros((n_ab, bsz, DN_HEADS, DN_DK, DN_DV), F32)
    gla0 = jnp.zeros((n_ab, bsz, GLA_HEADS, GLA_DK, GLA_DV), F32)
    y_p, conv_p, dn_p, gla_p, ck_p, cv_p = _trunk(x_prompt, conv0, dn0, gla0, None, None, w, True, 512)
    ts = x_sample.shape[0] * x_sample.shape[1]
    y_s, conv_s, dn_s, gla_s, ck_s, cv_s = _trunk(x_sample, state_dn_conv, state_dn, state_gla,
                                                  cache_c_k, cache_c_v, w, False, ts)
    return (y_p, y_s, conv_p, conv_s, dn_p, dn_s, gla_p, gla_s, ck_p, ck_s, cv_p, cv_s)
```

```python
import functools
import math

import numpy as np
import jax
import jax.numpy as jnp
from jax import lax
from jax.experimental import pallas as pl
from jax.experimental.pallas import tpu as pltpu

F32 = jnp.float32
BF16 = jnp.bfloat16

EPS = 1e-6
CHUNK = 64
PAST_LEN = 2048
D_MODEL = 1024
DN_HEADS = 8
DN_DK = 64
DN_DV = 64
CONV_W = 4
DN_QKV = DN_HEADS * (2 * DN_DK + DN_DV)
GLA_HEADS = 8
GLA_DK = 32
GLA_DV = 64
GLA_RANK = 16
GLA_NORMALIZER = 16.0
GLA_SUB = 16
C_HEADS = 16
C_HD = 64
C_BAND_CHUNKS = 8
C_PAST = C_BAND_CHUNKS * CHUNK
MAX_REL = 128
ATT_QB = 256
LOG2E = math.log2(math.e)
ATT_QSCALE = C_HD ** -0.5 * LOG2E

H = DN_HEADS
HW = DN_HEADS * DN_DV
GW = GLA_HEADS * GLA_DK
LANE = 128
SUBLANE = 8
NFG = HW // LANE
AB_STREAMS = 4

OFF_QKV = 0
OFF_DGATE = DN_QKV
OFF_GQ = OFF_DGATE + HW
OFF_GK = OFF_GQ + GW
OFF_GV = OFF_GK + GW
OFF_GGATE = OFF_GV + HW
OFF_SMALL = OFF_GGATE + HW
AB_PAD = OFF_SMALL + LANE

VMEM_LIMIT = 56 * 1024 * 1024


def _mm(a, b):
    return jnp.dot(a, b, preferred_element_type=F32)


def _mm_nt(a, b):
    return lax.dot_general(a, b, (((1,), (1,)), ((), ())), preferred_element_type=F32)


def _mm_tn(a, b):
    return lax.dot_general(a, b, (((0,), (0,)), ((), ())), preferred_element_type=F32)


def _split(x, n):
    parts = []
    r = x
    for t in range(n):
        p = r.astype(BF16)
        parts.append(p)
        if t + 1 < n:
            r = r - p.astype(F32)
    return parts


def _mmx(x, m, n):
    out = None
    for p in _split(x, n):
        t = _mm(p, m)
        out = t if out is None else out + t
    return out


def _mmx_left(m, x, n):
    out = None
    for p in _split(x, n):
        t = _mm(m, p)
        out = t if out is None else out + t
    return out


def _softplus(x):
    return jnp.maximum(x, 0.0) + jnp.log(1.0 + jnp.exp(-jnp.abs(x)))


def _tile_rows(x, n):
    return jnp.concatenate([x] * n, axis=0)


def _grp(x, g):
    return x[:, LANE * g:LANE * (g + 1)]


def _rms(x, g):
    ms = jnp.mean(x * x, axis=-1, keepdims=True)
    return (x * lax.rsqrt(ms + EPS)) * g


def _const_spec(shape):
    nd = len(shape)
    return pl.BlockSpec(shape, lambda *_: (0,) * nd, pipeline_mode=pl.Buffered(1))


def _pre_kernel(x_ref, g_ref, w_ref, o_ref, *, col_chunk):
    xn = _rms(x_ref[...], g_ref[...]).astype(BF16)
    n = o_ref.shape[-1]
    for c0 in range(0, n, col_chunk):
        c1 = min(n, c0 + col_chunk)
        o_ref[:, c0:c1] = _mm(xn, w_ref[:, c0:c1])


def _pre(x, g, w, tm):
    t, d = x.shape
    n = w.shape[1]
    return pl.pallas_call(
        functools.partial(_pre_kernel, col_chunk=512),
        grid=(t // tm,),
        in_specs=[pl.BlockSpec((tm, d), lambda i: (i, 0)),
                  _const_spec((1, d)),
                  _const_spec((d, n))],
        out_specs=pl.BlockSpec((tm, n), lambda i: (i, 0)),
        out_shape=jax.ShapeDtypeStruct((t, n), F32),
        compiler_params=pltpu.CompilerParams(dimension_semantics=("parallel",),
                                             vmem_limit_bytes=VMEM_LIMIT),
        name="pre_proj",
    )(x, g.reshape(1, d), w)


def _pre_qkv_kernel(x_ref, g_ref, w_ref, o_ref, k_ref, v_ref, *, col_chunk):
    xn = _rms(x_ref[...], g_ref[...]).astype(BF16)
    dm = k_ref.shape[-1]
    for c0 in range(0, 3 * dm, col_chunk):
        c1 = c0 + col_chunk
        r = _mm(xn, w_ref[:, c0:c1])
        if c1 <= dm:
            o_ref[:, c0:c1] = (r * ATT_QSCALE).astype(BF16)
        else:
            o_ref[:, c0:c1] = r.astype(BF16)
            kv_ref = k_ref if c1 <= 2 * dm else v_ref
            kv_ref[:, c0 % dm:c0 % dm + col_chunk] = r


def _pre_qkv(x, g, w, tm, rows_per_stream):
    t, d = x.shape
    dm = w.shape[1] // 3
    nb = max(rows_per_stream // tm, 1)
    tail = pl.BlockSpec((tm, dm), lambda i: (i // nb, 0))
    return pl.pallas_call(
        functools.partial(_pre_qkv_kernel, col_chunk=512),
        grid=(t // tm,),
        in_specs=[pl.BlockSpec((tm, d), lambda i: (i, 0)),
                  _const_spec((1, d)),
                  _const_spec((d, 3 * dm))],
        out_specs=[pl.BlockSpec((tm, 3 * dm), lambda i: (i, 0)), tail, tail],
        out_shape=[jax.ShapeDtypeStruct((t, 3 * dm), BF16),
                   jax.ShapeDtypeStruct((t // nb, dm), F32),
                   jax.ShapeDtypeStruct((t // nb, dm), F32)],
        compiler_params=pltpu.CompilerParams(dimension_semantics=("arbitrary",),
                                             vmem_limit_bytes=VMEM_LIMIT),
        name="pre_qkv",
    )(x, g.reshape(1, d), w)


def _post_kernel(o_ref, x_ref, wo_ref, g_ref, wgu_ref, wd_ref, gf_ref, out_ref, acc_ref, *, ff_chunk, final):
    x1 = x_ref[...] + _mm(o_ref[...], wo_ref[...])
    xn = _rms(x1, g_ref[...]).astype(BF16)
    dff = wd_ref.shape[0]
    for f0 in range(0, dff, ff_chunk):
        gate = _mm(xn, wgu_ref[:, f0:f0 + ff_chunk])
        up = _mm(xn, wgu_ref[:, dff + f0:dff + f0 + ff_chunk])
        a = (gate * jax.nn.sigmoid(gate) * up).astype(BF16)
        part = _mm(a, wd_ref[f0:f0 + ff_chunk, :])
        if f0 == 0:
            acc_ref[...] = part
        else:
            acc_ref[...] += part
    x2 = x1 + acc_ref[...]
    if final:
        x2 = _rms(x2, gf_ref[...])
    out_ref[...] = x2


def _post(o, x, wo, g, wgu, wd, gf, tm, final):
    t, d = x.shape
    dm = o.shape[1]
    dff = wd.shape[0]
    return pl.pallas_call(
        functools.partial(_post_kernel, ff_chunk=256, final=final),
        grid=(t // tm,),
        in_specs=[pl.BlockSpec((tm, dm), lambda i: (i, 0)),
                  pl.BlockSpec((tm, d), lambda i: (i, 0)),
                  _const_spec((dm, d)),
                  _const_spec((1, d)),
                  _const_spec((d, 2 * dff)),
                  _const_spec((dff, d)),
                  _const_spec((1, d))],
        out_specs=pl.BlockSpec((tm, d), lambda i: (i, 0)),
        out_shape=jax.ShapeDtypeStruct((t, d), F32),
        scratch_shapes=[pltpu.VMEM((tm, d), F32)],
        compiler_params=pltpu.CompilerParams(dimension_semantics=("parallel",),
                                             vmem_limit_bytes=VMEM_LIMIT),
        name="post_ffn",
    )(o, x, wo, g.reshape(1, d), wgu, wd, gf.reshape(1, d))


def _score_group_of_fg(fg, c):
    return (2 * fg * c) // LANE


@functools.lru_cache(maxsize=None)
def _gla_pairs(c):
    hs = LANE // c
    pairs = []
    for sg in range(H * c // LANE):
        for qd in range(GW // LANE):
            if set(range(sg * hs, (sg + 1) * hs)) & set(range(4 * qd, 4 * qd + 4)):
                pairs.append((sg, qd))
    return tuple(pairs)


@functools.lru_cache(maxsize=None)
def _ab_consts(c):
    w = H * c
    hs = LANE // c
    nsg = w // LANE
    nsb = c // GLA_SUB
    i = np.arange(c)
    r = np.arange(LANE)
    w_pos = np.arange(w) % c
    ltri = (i[None, :] <= i[:, None])
    eye_w = (i[:, None] == w_pos[None, :])
    causal_w = (w_pos[None, :] <= i[:, None])
    strict_w = (w_pos[None, :] < i[:, None])
    dmask = causal_w & ((i[:, None] // GLA_SUB) == (w_pos[None, :] // GLA_SUB))
    row_hl = r // c
    row_j = r % c
    m_ss = (row_hl[:, None] == row_hl[None, :])
    m_ff = ((r // DN_DV)[:, None] == (r // DN_DV)[None, :])
    m_sf = np.stack([(_score_group_of_fg(fg, c) * hs + row_hl)[:, None] == (2 * fg + r // DN_DV)[None, :]
                     for fg in range(NFG)])
    mg = np.stack([(sg * hs + row_hl)[:, None] == (r // 16)[None, :] for sg in range(nsg)])
    pairs = _gla_pairs(c)
    m_sq = np.stack([(sg * hs + row_hl)[:, None] == (4 * qd + r // GLA_DK)[None, :] for sg, qd in pairs])
    m_sq3 = np.stack([np.concatenate([m & ((row_j // GLA_SUB) == j)[:, None] for j in range(max(nsb - 1, 1))],
                                     axis=1) for m in m_sq])
    m_fq = np.stack([(2 * fg + r // DN_DV)[:, None] == (4 * (fg // 2) + r // GLA_DK)[None, :]
                     for fg in range(NFG)])
    sel = np.zeros((6, LANE, LANE), np.float32)
    cst = np.zeros((SUBLANE, LANE), np.float32)
    for h in range(H):
        for p in range(3):
            sel[p, h, 16 * h + p] = 1.0
            sel[3 + p, h, 16 * h + 3 + p] = -1.0
            cst[0, 16 * h + 3 + p] = 1.0
            cst[1, 16 * h + p] = 1.0
    expg = np.zeros((LANE, HW), np.float32)
    expb = np.zeros((LANE, HW), np.float32)
    for h in range(H):
        expg[h, DN_DV * h:DN_DV * (h + 1)] = 1.0
        expb[H + h, DN_DV * h:DN_DV * (h + 1)] = 1.0
    f = lambda a: jnp.asarray(a, F32)
    b = lambda a: jnp.asarray(a, BF16)
    return dict(ltri=b(ltri), eye_w=f(eye_w), causal_w=f(causal_w), strict_w=f(strict_w), dmask=f(dmask),
                m_ss=b(m_ss), m_sf=b(m_sf), m_ffb=b(m_ff), m_ff=f(m_ff), mg=b(mg), m_sq=b(m_sq), m_sq3=b(m_sq3),
                m_fq=f(m_fq), sel=b(sel), cst=f(cst), expg=b(expg), expb=b(expb))


_AB_CONST_ORDER = ("ltri", "eye_w", "causal_w", "strict_w", "dmask", "m_ss", "m_sf", "m_ffb", "m_ff", "mg",
                   "m_sq", "m_sq3", "m_fq", "sel", "cst", "expg", "expb")


def _headsum(xs, bones):
    rows = xs[0].shape[0]
    pieces = [_grp(p, g) for x in xs for p in _split(x, 2) for g in range(NFG)]
    y = _mm(jnp.concatenate(pieces, axis=0), bones)
    outs = []
    for n in range(len(xs)):
        base = 2 * NFG * n
        outs.append(jnp.concatenate(
            [y[(base + g) * rows:(base + g + 1) * rows] + y[(base + NFG + g) * rows:(base + NFG + g + 1) * rows]
             for g in range(NFG)], axis=1))
    return outs


def _ab_kernel(proj_ref, conv0_ref, dn0_ref, gl0_ref, convw_ref, gparam_ref, w2_ref, gkb_ref, onorm_ref,
               ltri_ref, eye_ref, causal_ref, strict_ref, dmask_ref, mss_ref, msf_ref, mffb_ref, mff_ref, mg_ref,
               msq_ref, msq3_ref, mfq_ref, sel_ref, cst_ref, expg_ref, expb_ref,
               o_ref, convn_ref, dns_ref, gls_ref,
               xp_ref, sdn_ref, sgl_ref, *, c, nc, ns):
    i = pl.program_id(1)
    nsb = c // GLA_SUB
    hs = LANE // c
    nsg = H * c // LANE
    sg_of = [_score_group_of_fg(fg, c) for fg in range(NFG)]
    streams = range(ns)

    @pl.when(i == 0)
    def _load_state():
        xp_ref[:, 0:SUBLANE, :] = conv0_ref[...]
        sdn_ref[...] = dn0_ref[...]
        sgl_ref[...] = gl0_ref[...]

    cw = convw_ref[...]
    bones = mffb_ref[...]
    mss = mss_ref[...]
    ltri = ltri_ref[...]
    gp = gparam_ref[...]
    onorm = onorm_ref[...]

    q, k, v = [], [], []
    for s in streams:
        x = proj_ref[s, :, OFF_QKV:OFF_QKV + DN_QKV]
        xp_ref[s, SUBLANE:SUBLANE + c, :] = x
        xa = xp_ref[s]
        y = pltpu.roll(xa, c + 3, axis=0)[0:c] * cw[0:1, :]
        y = y + pltpu.roll(xa, c + 2, axis=0)[0:c] * cw[1:2, :]
        y = y + pltpu.roll(xa, c + 1, axis=0)[0:c] * cw[2:3, :]
        y = y + x * cw[3:4, :]
        xp_ref[s, 0:SUBLANE, :] = x[c - SUBLANE:c, :]
        qkv = y * jax.nn.sigmoid(y)
        q.append(qkv[:, 0:HW])
        k.append(qkv[:, HW:2 * HW])
        v.append(qkv[:, 2 * HW:3 * HW])

    for s in streams:
        sq, sk = _headsum([q[s] * q[s], k[s] * k[s]], bones)
        q[s] = q[s] * lax.rsqrt(sq + EPS) * (DN_DK ** -0.5)
        k[s] = k[s] * lax.rsqrt(sk + EPS)

    pairs = _gla_pairs(c)
    small = [proj_ref[s, :, OFF_SMALL:OFF_SMALL + LANE] for s in streams]
    gl = {}

    def gla_gate_logits():
        gl["z"] = [_mm(small[s].astype(BF16), w2_ref[...]) + gkb_ref[...] for s in streams]

    def gla_cumsum():
        gl["bcum"] = [_mmx_left(ltri, -_softplus(-gl["z"][s]) * (1.0 / GLA_NORMALIZER), 3) for s in streams]

    def gla_scores_and_state():
        gl["a"], gl["o_inter"], gl["v2b"] = [], [], []
        for s in streams:
            bcum = gl["bcum"][s]
            q2 = proj_ref[s, :, OFF_GQ:OFF_GQ + GW] * (GLA_DK ** -0.5)
            k2 = proj_ref[s, :, OFF_GK:OFF_GK + GW]
            v2b = proj_ref[s, :, OFF_GV:OFF_GV + HW].astype(BF16)
            blast = bcum[c - 1:c, :]
            qe = (q2 * jnp.exp(bcum)).astype(BF16)
            kdec = (k2 * jnp.exp(blast - bcum)).astype(BF16)
            eblast = jnp.exp(blast)
            rmid = jnp.concatenate(
                [jnp.broadcast_to(bcum[GLA_SUB * t + GLA_SUB // 2:GLA_SUB * t + GLA_SUB // 2 + 1, :], (GLA_SUB, GW))
                 for t in range(nsb)], axis=0)
            qm = (q2 * jnp.exp(bcum - rmid)).astype(BF16)
            km = (k2 * jnp.exp(rmid - bcum)).astype(BF16)
            parts = [None] * nsg
            for n, (sg, qd) in enumerate(pairs):
                t = _mm_nt(_grp(qm, qd), _tile_rows(_grp(km, qd), hs) * msq_ref[n])
                parts[sg] = t if parts[sg] is None else parts[sg] + t
            a_gl = dmask_ref[...] * jnp.concatenate(parts, axis=1)
            if nsb > 1:
                rend = jnp.concatenate(
                    [jnp.broadcast_to(bcum[GLA_SUB * (t + 1) - 1:GLA_SUB * (t + 1), :], (GLA_SUB, GW))
                     for t in range(nsb)], axis=0)
                kr = (k2 * jnp.exp(rend - bcum)).astype(BF16)
                rowi = lax.broadcasted_iota(jnp.int32, (c, GW), 0)
                qs = []
                for t in range(nsb - 1):
                    e = jnp.where(rowi >= GLA_SUB * (t + 1), bcum - rend[GLA_SUB * t:GLA_SUB * t + 1, :], -jnp.inf)
                    qs.append((q2 * jnp.exp(e)).astype(BF16))
                parts = [None] * nsg
                for n, (sg, qd) in enumerate(pairs):
                    qcat = jnp.concatenate([_grp(x, qd) for x in qs], axis=1)
                    kcat = _tile_rows(jnp.concatenate([_grp(kr, qd)] * (nsb - 1), axis=1), hs) * msq3_ref[n]
                    t = _mm_nt(qcat, kcat)
                    parts[sg] = t if parts[sg] is None else parts[sg] + t
                a_gl = a_gl + jnp.concatenate(parts, axis=1)
            o_inter = []
            for fg in range(NFG):
                qd = fg // 2
                st = sgl_ref[s, fg]
                o_inter.append(_mm_nt(_grp(qe, qd), st.astype(BF16)))
                sgl_ref[s, fg] = st * _grp(eblast, qd) + _mm_tn(_grp(v2b, fg), _grp(kdec, qd)) * mfq_ref[fg]
            gl["a"].append(a_gl.astype(BF16))
            gl["o_inter"].append(o_inter)
            gl["v2b"].append(v2b)

    def gla_out():
        gl["o"] = [jnp.concatenate(
            [gl["o_inter"][s][fg]
             + _mm(_grp(gl["a"][s], sg_of[fg]), _tile_rows(_grp(gl["v2b"][s], fg), hs) * msf_ref[fg])
             for fg in range(NFG)], axis=1) for s in streams]

    gla_gate_logits()
    gcum_parts = []
    for s in streams:
        g_full = -jnp.exp(gp[0:1, :]) * _softplus(small[s] + gp[1:2, :])
        gcum_parts.append(_split(_mmx_left(ltri, g_full, 3), 3))
    gla_cumsum()
    beta_w, gcum_w, g1, g2 = [], [], [], []
    for s in streams:
        gw, a1, a2 = None, cst_ref[0:1, :], cst_ref[1:2, :]
        for p in range(3):
            t = _mm(gcum_parts[s][p], expg_ref[...])
            gw = t if gw is None else gw + t
            a1 = a1 + _mm(gcum_parts[s][p], sel_ref[p])
            a2 = a2 + _mm(gcum_parts[s][p], sel_ref[3 + p])
        gcum_w.append(gw)
        g1.append(a1.astype(BF16))
        g2.append(a2.astype(BF16))
        beta_w.append(_mmx(jax.nn.sigmoid(small[s]), expb_ref[...], 2))
    eg_w = [jnp.exp(gcum_w[s]) for s in streams]
    edec_w = [jnp.exp(gcum_w[s][c - 1:c, :] - gcum_w[s]) for s in streams]
    eglast_w = [jnp.exp(gcum_w[s][c - 1:c, :]) for s in streams]
    gla_scores_and_state()

    decay_w = []
    for s in streams:
        d_w = jnp.concatenate([_mm_nt(g1[s], _tile_rows(g2[s], hs) * mg_ref[sg]) for sg in range(nsg)], axis=1)
        decay_w.append(jnp.exp(jnp.where(causal_ref[...] > 0.0, d_w, -jnp.inf)))

    kb = [k[s] * beta_w[s] for s in streams]
    a_w, qkd_b = [], []
    for s in streams:
        kq = jnp.concatenate([kb[s], q[s]], axis=0).astype(BF16)
        kbf = k[s].astype(BF16)
        parts = [None] * nsg
        for fg in range(NFG):
            t = _mm_nt(_grp(kq, fg), _tile_rows(_grp(kbf, fg), hs) * msf_ref[fg])
            parts[sg_of[fg]] = t if parts[sg_of[fg]] is None else parts[sg_of[fg]] + t
        r = jnp.concatenate(parts, axis=1)
        a_w.append(strict_ref[...] * r[0:c] * decay_w[s])
        qkd_b.append((r[c:2 * c] * decay_w[s]).astype(BF16))

    def ssprod(x_w, y_w):
        xb = x_w.astype(BF16)
        yb = y_w.astype(BF16)
        return jnp.concatenate([_mm(_grp(xb, sg), _tile_rows(_grp(yb, sg), hs) * mss) for sg in range(nsg)], axis=1)

    nlev = int(math.log2(c))
    pw = [-a_w[s] for s in streams]
    t_w = [eye_ref[...] + pw[s] for s in streams]
    pw = [ssprod(pw[s], pw[s]) for s in streams]
    gla_out()
    for _ in range(2, nlev):
        rr = [ssprod(jnp.concatenate([pw[s], t_w[s]], axis=0), pw[s]) for s in streams]
        t_w = [t_w[s] + rr[s][c:2 * c] for s in streams]
        pw = [rr[s][0:c] for s in streams]
    rr = [ssprod(t_w[s], pw[s]) for s in streams]
    tb = [(t_w[s] + rr[s]).astype(BF16) for s in streams]

    vbb = [(v[s] * beta_w[s]).astype(BF16) for s in streams]
    kbe = [(kb[s] * eg_w[s]).astype(BF16) for s in streams]
    qeg = [q[s] * eg_w[s] for s in streams]
    kdc = [(k[s] * edec_w[s]).astype(BF16) for s in streams]
    sf_pairs = [(s, fg) for fg in range(NFG) for s in streams]
    uw, rs, s_dn = {}, {}, {}
    for s, fg in sf_pairs:
        msf = msf_ref[fg]
        rhs = jnp.concatenate([_tile_rows(_grp(vbb[s], fg), hs) * msf, _tile_rows(_grp(kbe[s], fg), hs) * msf], axis=1)
        uw[s, fg] = _mm(_grp(tb[s], sg_of[fg]), rhs)
    for s, fg in sf_pairs:
        s_dn[s, fg] = sdn_ref[s, fg]
        lhs = jnp.concatenate([uw[s, fg][:, LANE:2 * LANE], _grp(qeg[s], fg)], axis=0).astype(BF16)
        rs[s, fg] = _mm(lhs, s_dn[s, fg].astype(BF16))
    o_dn = [[None] * NFG for _ in streams]
    for s, fg in sf_pairs:
        vnb = (uw[s, fg][:, 0:LANE] - rs[s, fg][0:c]).astype(BF16)
        o_dn[s][fg] = rs[s, fg][c:2 * c] + _mm(_grp(qkd_b[s], sg_of[fg]), _tile_rows(vnb, hs) * msf_ref[fg])
        sdn_ref[s, fg] = s_dn[s, fg] * _grp(eglast_w[s], fg) + _mm_tn(_grp(kdc[s], fg), vnb) * mff_ref[...]
    o_dn = [jnp.concatenate(o_dn[s], axis=1) for s in streams]
    o_gl = gl["o"]

    for s in streams:
        md, mgl = _headsum([o_dn[s] * o_dn[s], o_gl[s] * o_gl[s]], bones)
        gd = proj_ref[s, :, OFF_DGATE:OFF_DGATE + HW]
        gg = proj_ref[s, :, OFF_GGATE:OFF_GGATE + HW]
        y_dn = (o_dn[s] * lax.rsqrt(md * (1.0 / DN_DV) + EPS)) * onorm[0:1, :] * (gd * jax.nn.sigmoid(gd))
        y_gl = (o_gl[s] * lax.rsqrt(mgl * (1.0 / GLA_DV) + EPS)) * onorm[1:2, :] * (gg * jax.nn.sigmoid(gg))
        o_ref[s, :, 0:HW] = y_dn.astype(BF16)
        o_ref[s, :, HW:2 * HW] = y_gl.astype(BF16)

    @pl.when(i == nc - 1)
    def _store_state():
        convn_ref[...] = xp_ref[:, 0:SUBLANE, :]
        dns_ref[...] = sdn_ref[...]
        gls_ref[...] = sgl_ref[...]


def _ab_mixer(proj, conv0, dn0, gl0, convw, gparam, w2p, gkb, onorm, c):
    b, l, _ = proj.shape
    nc = l // c
    ns = AB_STREAMS
    consts = _ab_consts(c)
    cvals = [consts[n] for n in _AB_CONST_ORDER]
    per_b = lambda shape: pl.BlockSpec((ns,) + shape, lambda bi, ci: (bi,) + (0,) * len(shape))
    state = (NFG, LANE, LANE)
    in_specs = ([pl.BlockSpec((ns, c, AB_PAD), lambda bi, ci: (bi, ci, 0)),
                 per_b((SUBLANE, DN_QKV)), per_b(state), per_b(state),
                 _const_spec(convw.shape), _const_spec(gparam.shape), _const_spec(w2p.shape),
                 _const_spec(gkb.shape), _const_spec(onorm.shape)]
                + [_const_spec(a.shape) for a in cvals])
    out_specs = [pl.BlockSpec((ns, c, 2 * HW), lambda bi, ci: (bi, ci, 0)),
                 per_b((SUBLANE, DN_QKV)), per_b(state), per_b(state)]
    out_shape = [jax.ShapeDtypeStruct((b, l, 2 * HW), BF16),
                 jax.ShapeDtypeStruct((b, SUBLANE, DN_QKV), F32),
                 jax.ShapeDtypeStruct((b,) + state, F32),
                 jax.ShapeDtypeStruct((b,) + state, F32)]
    return pl.pallas_call(
        functools.partial(_ab_kernel, c=c, nc=nc, ns=ns),
        grid=(b // ns, nc),
        in_specs=in_specs,
        out_specs=out_specs,
        out_shape=out_shape,
        scratch_shapes=[pltpu.VMEM((ns, SUBLANE + c, DN_QKV), F32),
                        pltpu.VMEM((ns,) + state, F32),
                        pltpu.VMEM((ns,) + state, F32)],
        compiler_params=pltpu.CompilerParams(dimension_semantics=("parallel", "arbitrary"),
                                             vmem_limit_bytes=VMEM_LIMIT),
        name="ab_mixer",
    )(proj, conv0, dn0, gl0, convw, gparam, w2p, gkb, onorm, *cvals)


def _dn_state_to_groups(s):
    b = s.shape[0]
    out = jnp.zeros((b, NFG, LANE, LANE), s.dtype)
    for h in range(DN_HEADS):
        hl = h % 2
        out = out.at[:, h // 2, DN_DK * hl:DN_DK * (hl + 1), DN_DV * hl:DN_DV * (hl + 1)].set(s[:, h])
    return out


def _dn_state_from_groups(g):
    return jnp.stack([g[:, h // 2, DN_DK * (h % 2):DN_DK * (h % 2 + 1), DN_DV * (h % 2):DN_DV * (h % 2 + 1)]
                      for h in range(DN_HEADS)], axis=1)


def _gla_state_to_groups(s):
    b = s.shape[0]
    out = jnp.zeros((b, NFG, LANE, LANE), s.dtype)
    for h in range(GLA_HEADS):
        hl, hq = h % 2, h % 4
        out = out.at[:, h // 2, GLA_DV * hl:GLA_DV * (hl + 1), GLA_DK * hq:GLA_DK * (hq + 1)].set(
            jnp.swapaxes(s[:, h], -1, -2))
    return out


def _gla_state_from_groups(g):
    return jnp.stack([jnp.swapaxes(g[:, h // 2, GLA_DV * (h % 2):GLA_DV * (h % 2 + 1),
                                     GLA_DK * (h % 4):GLA_DK * (h % 4 + 1)], -1, -2)
                      for h in range(GLA_HEADS)], axis=1)


def _softmax2_rows(parts):
    m = None
    for s in parts:
        t = jnp.max(s, axis=-1, keepdims=True)
        m = t if m is None else jnp.maximum(m, t)
    es = [jnp.exp2(s - m) for s in parts]
    l = None
    for e in es:
        t = jnp.sum(e, axis=-1, keepdims=True)
        l = t if l is None else l + t
    inv = 1.0 / l
    return [e * inv for e in es]


def _fold_lanes(x, op):
    out = x[:, 0:LANE]
    for c0 in range(LANE, x.shape[1], LANE):
        out = op(out, x[:, c0:c0 + LANE])
    return out


def _attn_prompt_kernel(q_ref, k0_ref, k1_ref, k2_ref, v0_ref, v1_ref, v2_ref, bias_ref, o_ref, *, qb):
    i = pl.program_id(1)
    offs = (jnp.where(i >= 2, 0.0, -jnp.inf), jnp.where(i >= 1, 0.0, -jnp.inf), 0.0)
    k_refs = (k0_ref, k1_ref, k2_ref)
    v_refs = (v0_ref, v1_ref, v2_ref)

    nq = qb // CHUNK
    tpb = qb // LANE
    cpt = LANE // CHUNK

    def tile_in_band(qc, lt):
        return cpt * lt + cpt - 1 >= qc + 2 * nq - C_BAND_CHUNKS and cpt * lt <= qc + 2 * nq

    def scores(h):
        sl = slice(C_HD * h, C_HD * (h + 1))
        qh = q_ref[:, sl]
        return [_mm_nt(qh, k_refs[t][:, sl]) for t in range(3)]

    def finish(h, ss):
        sl = slice(C_HD * h, C_HD * (h + 1))
        s_t, ems = {}, []
        for qc in range(nq):
            rows = slice(CHUNK * qc, CHUNK * (qc + 1))
            em = None
            for t in range(3):
                et = None
                for j in range(tpb):
                    lt = t * tpb + j
                    if tile_in_band(qc, lt):
                        s = ss[t][rows, LANE * j:LANE * (j + 1)] + bias_ref[h, rows, LANE * lt:LANE * (lt + 1)]
                        s_t[qc, lt] = s
                        et = s if et is None else jnp.maximum(et, s)
                if et is not None:
                    et = et + offs[t]
                    em = et if em is None else jnp.maximum(em, et)
            ems.append(em)
        m = jnp.max(jnp.concatenate(ems, axis=0), axis=-1, keepdims=True)
        shift = [m - offs[t] for t in range(3)]
        p_tiles, els = [], []
        for qc in range(nq):
            rows = slice(CHUNK * qc, CHUNK * (qc + 1))
            el, row_tiles = None, []
            for lt in range(3 * tpb):
                if (qc, lt) in s_t:
                    p = jnp.exp2(s_t[qc, lt] - shift[lt // tpb][rows])
                    el = p if el is None else el + p
                    row_tiles.append(p.astype(BF16))
                else:
                    row_tiles.append(jnp.zeros((CHUNK, LANE), BF16))
            els.append(el)
            p_tiles.append(row_tiles)
        inv_l = [1.0 / jnp.sum(jnp.concatenate(els, axis=0), axis=-1, keepdims=True)]
        acc = None
        for t in range(3):
            p_blk = jnp.concatenate(
                [jnp.concatenate(p_tiles[qc][t * tpb:(t + 1) * tpb], axis=1) for qc in range(nq)], axis=0)
            at = _mm(p_blk, v_refs[t][:, sl])
            acc = at if acc is None else acc + at
        o_ref[:, sl] = (acc * jnp.concatenate(inv_l, axis=0)).astype(BF16)

    ss_next = scores(0)
    for h in range(C_HEADS):
        ss = ss_next
        if h + 1 < C_HEADS:
            ss_next = scores(h + 1)
        finish(h, ss)


def _attn_prompt(qkv, bias, qb):
    b, l, _ = qkv.shape
    dm = C_HEADS * C_HD
    kv_spec = lambda colblk, back: pl.BlockSpec(
        (None, qb, dm), lambda bi, i: (bi, jnp.maximum(i - back, 0), colblk))
    return pl.pallas_call(
        functools.partial(_attn_prompt_kernel, qb=qb),
        grid=(b, l // qb),
        in_specs=[pl.BlockSpec((None, qb, dm), lambda bi, i: (bi, i, 0)),
                  kv_spec(1, 2), kv_spec(1, 1), kv_spec(1, 0),
                  kv_spec(2, 2), kv_spec(2, 1), kv_spec(2, 0),
                  _const_spec(bias.shape)],
        out_specs=pl.BlockSpec((None, qb, dm), lambda bi, i: (bi, i, 0)),
        out_shape=jax.ShapeDtypeStruct((b, l, dm), BF16),
        compiler_params=pltpu.CompilerParams(dimension_semantics=("parallel", "parallel"),
                                             vmem_limit_bytes=VMEM_LIMIT),
        name="attn_prompt",
    )(qkv, qkv, qkv, qkv, qkv, qkv, qkv, bias)


def _attn_sample_kernel(q_ref, k_ref, v_ref, ck_ref, cv_ref, bc_ref, bn_ref, o_ref):
    for h in range(C_HEADS):
        sl = slice(C_HD * h, C_HD * (h + 1))
        qh = q_ref[:, sl]
        sc = _mm_nt(qh, ck_ref[:, sl].astype(BF16)) + bc_ref[h]
        sn = _mm_nt(qh, k_ref[:, sl]) + bn_ref[h]
        pc, pn = _softmax2_rows([sc, sn])
        oh = _mm(pc.astype(BF16), cv_ref[:, sl].astype(BF16)) + _mm(pn.astype(BF16), v_ref[:, sl])
        o_ref[:, sl] = oh.astype(BF16)


def _attn_sample(qkv, ck, cv, bias_c, bias_n):
    b, l, _ = qkv.shape
    dm = C_HEADS * C_HD
    ncache = ck.shape[1]
    new_spec = lambda colblk: pl.BlockSpec((None, l, dm), lambda bi: (bi, 0, colblk))
    cache_spec = pl.BlockSpec((None, ncache, dm), lambda bi: (bi, 0, 0))
    return pl.pallas_call(
        _attn_sample_kernel,
        grid=(b,),
        in_specs=[new_spec(0), new_spec(1), new_spec(2), cache_spec, cache_spec,
                  _const_spec(bias_c.shape), _const_spec(bias_n.shape)],
        out_specs=pl.BlockSpec((None, l, dm), lambda bi: (bi, 0, 0)),
        out_shape=jax.ShapeDtypeStruct((b, l, dm), BF16),
        compiler_params=pltpu.CompilerParams(dimension_semantics=("parallel",),
                                             vmem_limit_bytes=VMEM_LIMIT),
        name="attn_sample",
    )(qkv, qkv, qkv, ck, cv, bias_c, bias_n)


def _prompt_bias(table, qb):
    nh = table.shape[0]
    rows, cols = qb, 3 * qb
    period = rows + cols
    far_past = jnp.broadcast_to(table[:, 2 * MAX_REL:], (nh, 3 * qb - 1 - MAX_REL))
    far_future = jnp.broadcast_to(table[:, :1], (nh, qb - 1 - MAX_REL))
    w = jnp.concatenate([far_past, table[:, ::-1], far_future, jnp.zeros((nh, 1), table.dtype)], axis=1)
    w = jnp.roll(w, -(rows - 1), axis=1)
    bias = jnp.tile(w, (1, rows))[:, :rows * (period - 1)].reshape(nh, rows, period - 1)[:, :, :cols]
    rq = np.arange(rows)[:, None]
    rk = np.arange(cols)[None, :]
    qc = rq // CHUNK
    kc = rk // CHUNK - (2 * qb) // CHUNK
    band = (kc <= qc) & (kc >= qc - C_BAND_CHUNKS)
    return jnp.where(jnp.asarray(band)[None], bias.astype(F32) * LOG2E, -jnp.inf)


def _sample_bias(table, l, ncache):
    qpos = PAST_LEN + np.arange(l)
    kpos = np.concatenate([PAST_LEN - ncache + np.arange(ncache), PAST_LEN + np.arange(l)])
    rel = np.clip(qpos[:, None] - kpos[None, :], -MAX_REL, MAX_REL) + MAX_REL
    qch = qpos // CHUNK
    kch = kpos // CHUNK
    valid = (kch[None, :] <= qch[:, None]) & (kch[None, :] >= qch[:, None] - C_BAND_CHUNKS)
    bias = jnp.where(jnp.asarray(valid)[None], table[:, rel].astype(F32) * LOG2E, -jnp.inf)
    return bias[:, :, :ncache], bias[:, :, ncache:]


def _permute_w_in(w):
    d = w.shape[0]
    o_a = DN_QKV
    o_b = o_a + DN_HEADS
    o_gate = o_b + DN_HEADS
    o_gq = o_gate + HW
    o_gk = o_gq + GW
    o_gv = o_gk + GW
    o_lr = o_gv + HW
    o_gg = o_lr + GLA_RANK
    return jnp.concatenate(
        [w[:, :DN_QKV], w[:, o_gate:o_gq], w[:, o_gq:o_gk], w[:, o_gk:o_gv], w[:, o_gv:o_lr],
         w[:, o_gg:o_gg + HW], w[:, o_a:o_gate], w[:, o_lr:o_gg],
         jnp.zeros((d, LANE - 2 * DN_HEADS - GLA_RANK), w.dtype)], axis=1)


def _trunk(x, conv0, dn0, gla0, ck, cv, w, prompt, tm):
    b, l, d = x.shape
    depth = w["ffn_norm"].shape[0]
    c = min(CHUNK, l)
    xf = x.reshape(b * l, d)
    convs, dns, glas, ks, vs = [], [], [], [], []
    for layer in range(depth):
        i = layer // 2
        if layer % 2 == 0:
            proj = _pre(xf, w["ab_norm"][i], w["ab_w_in"][i], tm).reshape(b, l, AB_PAD)
            conv0p = jnp.pad(conv0[i], ((0, 0), (SUBLANE - (CONV_W - 1), 0), (0, 0)))
            o, convn, dn_s, gl_s = _ab_mixer(proj, conv0p, _dn_state_to_groups(dn0[i]),
                                             _gla_state_to_groups(gla0[i]), w["dn_conv_w"][i], w["gparam"][i],
                                             w["gk_w2p"][i], w["gla_gk_b"][i], w["onorm"][i], c)
            convs.append(convn[:, SUBLANE - (CONV_W - 1):, :])
            dns.append(_dn_state_from_groups(dn_s))
            glas.append(_gla_state_from_groups(gl_s))
            wo = w["ab_w_out"][i]
        else:
            dm = C_HEADS * C_HD
            rows = min(C_PAST, l) if prompt else l
            assert l <= tm or rows == tm, (l, tm, rows)
            qkv, k_rows, v_rows = _pre_qkv(xf, w["c_norm"][i], w["c_w_qkv"][i], tm, l)
            qkv = qkv.reshape(b, l, 3 * dm)
            ks.append(k_rows.reshape(b, -1, C_HEADS, C_HD)[:, -rows:])
            vs.append(v_rows.reshape(b, -1, C_HEADS, C_HD)[:, -rows:])
            if prompt:
                o = _attn_prompt(qkv, _prompt_bias(w["c_rel_bias"][i], ATT_QB), ATT_QB)
            else:
                ncache = ck.shape[2]
                bias_c, bias_n = _sample_bias(w["c_rel_bias"][i], l, ncache)
                o = _attn_sample(qkv, ck[i].reshape(b, ncache, dm), cv[i].reshape(b, ncache, dm), bias_c, bias_n)
            wo = w["c_w_out"][i]
        xf = _post(o.reshape(b * l, -1), xf, wo, w["ffn_norm"][layer], w["ffn_w_gu"][layer],
                   w["ffn_w_down"][layer], w["final_norm"], tm, final=(layer == depth - 1))
    return (xf.reshape(b, l, d), jnp.stack(convs), jnp.stack(dns), jnp.stack(glas), jnp.stack(ks), jnp.stack(vs))


def kernel(x_prompt, x_sample, state_dn_conv, state_dn, state_gla, cache_c_k, cache_c_v, ab_norm, ab_w_in, dn_conv_w, dn_a_log, dn_dt_bias, dn_out_norm, gla_gk_w2, gla_gk_b, gla_out_norm, ab_w_out, c_norm, c_w_qkv, c_rel_bias, c_w_out, ffn_norm, ffn_w_gu, ffn_w_down, final_norm):
    n_ab = ab_w_in.shape[0]
    bsz = x_prompt.shape[0]
    gparam = jnp.zeros((n_ab, SUBLANE, LANE), F32)
    gparam = gparam.at[:, 0, :DN_HEADS].set(dn_a_log).at[:, 1, :DN_HEADS].set(dn_dt_bias)
    gk_w2p = jnp.zeros((n_ab, LANE, GW), F32).at[:, 2 * DN_HEADS:2 * DN_HEADS + GLA_RANK, :].set(gla_gk_w2)
    onorm = jnp.stack([jnp.tile(dn_out_norm, (1, DN_HEADS)), jnp.tile(gla_out_norm, (1, GLA_HEADS))], axis=1)
    w = {
        "ab_norm": ab_norm, "c_norm": c_norm, "ffn_norm": ffn_norm, "final_norm": final_norm,
        "ab_w_in": jnp.stack([_permute_w_in(ab_w_in[i]) for i in range(n_ab)]).astype(BF16),
        "dn_conv_w": dn_conv_w, "gparam": gparam, "gk_w2p": gk_w2p.astype(BF16),
        "gla_gk_b": gla_gk_b[:, None, :], "onorm": onorm,
        "ab_w_out": ab_w_out.astype(BF16), "c_w_qkv": c_w_qkv.astype(BF16), "c_rel_bias": c_rel_bias,
        "c_w_out": c_w_out.astype(BF16), "ffn_w_gu": ffn_w_gu.astype(BF16), "ffn_w_down": ffn_w_down.astype(BF16),
    }
    conv0 = jnp.zeros((n_ab, bsz, CONV_W - 1, DN_QKV), F32)
    dn0 = jnp.zeros((n_ab, bsz, DN_HEADS, DN_DK, DN_DV), F32)
    gla0 = jnp.zeros((n_ab, bsz, GLA_HEADS, GLA_DK, GLA_DV), F32)
    y_p, conv_p, dn_p, gla_p, ck_p, cv_p = _trunk(x_prompt, conv0, dn0, gla0, None, None, w, True, 512)
    ts = x_sample.shape[0] * x_sample.shape[1]
    y_s, conv_s, dn_s, gla_s, ck_s, cv_s = _trunk(x_sample, state_dn_conv, state_dn, state_gla,
                                                  cache_c_k, cache_c_v, w, False, ts)
    return (y_p, y_s, conv_p, conv_s, dn_p, dn_s, gla_p, gla_s, ck_p, ck_s, cv_p, cv_s)
```

```python
import functools
import math

import numpy as np
import jax
import jax.numpy as jnp
from jax import lax
from jax.experimental import pallas as pl
from jax.experimental.pallas import tpu as pltpu

F32 = jnp.float32
BF16 = jnp.bfloat16

EPS = 1e-6
CHUNK = 64
PAST_LEN = 2048
D_MODEL = 1024
DN_HEADS = 8
DN_DK = 64
DN_DV = 64
CONV_W = 4
DN_QKV = DN_HEADS * (2 * DN_DK + DN_DV)
GLA_HEADS = 8
GLA_DK = 32
GLA_DV = 64
GLA_RANK = 16
GLA_NORMALIZER = 16.0
GLA_SUB = 16
C_HEADS = 16
C_HD = 64
C_BAND_CHUNKS = 8
C_PAST = C_BAND_CHUNKS * CHUNK
MAX_REL = 128
ATT_QB = 256
LOG2E = math.log2(math.e)
ATT_QSCALE = C_HD ** -0.5 * LOG2E

H = DN_HEADS
HW = DN_HEADS * DN_DV
GW = GLA_HEADS * GLA_DK
LANE = 128
SUBLANE = 8
NFG = HW // LANE
AB_STREAMS = 4

OFF_QKV = 0
OFF_DGATE = DN_QKV
OFF_GQ = OFF_DGATE + HW
OFF_GK = OFF_GQ + GW
OFF_GV = OFF_GK + GW
OFF_GGATE = OFF_GV + HW
OFF_SMALL = OFF_GGATE + HW
AB_PAD = OFF_SMALL + LANE

VMEM_LIMIT = 56 * 1024 * 1024


def _mm(a, b):
    return jnp.dot(a, b, preferred_element_type=F32)


def _mm_nt(a, b):
    return lax.dot_general(a, b, (((1,), (1,)), ((), ())), preferred_element_type=F32)


def _mm_tn(a, b):
    return lax.dot_general(a, b, (((0,), (0,)), ((), ())), preferred_element_type=F32)


def _split(x, n):
    parts = []
    r = x
    for t in range(n):
        p = r.astype(BF16)
        parts.append(p)
        if t + 1 < n:
            r = r - p.astype(F32)
    return parts


def _mmx(x, m, n):
    out = None
    for p in _split(x, n):
        t = _mm(p, m)
        out = t if out is None else out + t
    return out


def _mmx_left(m, x, n):
    out = None
    for p in _split(x, n):
        t = _mm(m, p)
        out = t if out is None else out + t
    return out


def _softplus(x):
    return jnp.maximum(x, 0.0) + jnp.log(1.0 + jnp.exp(-jnp.abs(x)))


def _tile_rows(x, n):
    return jnp.concatenate([x] * n, axis=0)


def _grp(x, g):
    return x[:, LANE * g:LANE * (g + 1)]


def _group_mm(lhs, rhs, nt=False):
    return [_mm_nt(a, b) if nt else _mm(a, b) for a, b in zip(lhs, rhs)]


def _rms(x, g):
    ms = jnp.mean(x * x, axis=-1, keepdims=True)
    return (x * lax.rsqrt(ms + EPS)) * g


def _const_spec(shape):
    nd = len(shape)
    return pl.BlockSpec(shape, lambda *_: (0,) * nd, pipeline_mode=pl.Buffered(1))


def _pre_kernel(x_ref, g_ref, w_ref, o_ref, *, col_chunk):
    xn = _rms(x_ref[...], g_ref[...]).astype(BF16)
    n = o_ref.shape[-1]
    for c0 in range(0, n, col_chunk):
        c1 = min(n, c0 + col_chunk)
        o_ref[:, c0:c1] = _mm(xn, w_ref[:, c0:c1])


def _pre(x, g, w, tm):
    t, d = x.shape
    n = w.shape[1]
    return pl.pallas_call(
        functools.partial(_pre_kernel, col_chunk=512),
        grid=(t // tm,),
        in_specs=[pl.BlockSpec((tm, d), lambda i: (i, 0)),
                  _const_spec((1, d)),
                  _const_spec((d, n))],
        out_specs=pl.BlockSpec((tm, n), lambda i: (i, 0)),
        out_shape=jax.ShapeDtypeStruct((t, n), F32),
        compiler_params=pltpu.CompilerParams(dimension_semantics=("parallel",),
                                             vmem_limit_bytes=VMEM_LIMIT),
        name="pre_proj",
    )(x, g.reshape(1, d), w)


def _silu(x):
    return x * jax.nn.sigmoid(x)


_AB_SEGMENTS = ((OFF_DGATE, HW, "silu"), (OFF_GQ, GW, "gla_q_scale"), (OFF_GK, GW, None), (OFF_GV, HW, None),
                (OFF_GGATE, HW, "silu"), (OFF_SMALL, LANE, None))


def _pre_ab_kernel(x_ref, g_ref, w_ref, conv0_ref, cw_ref, o_ref, convn_ref, xp_ref, *, nb, col_chunk):
    i = pl.program_id(0)
    tm = x_ref.shape[0]
    n = SUBLANE + tm
    xn = _rms(x_ref[...], g_ref[...]).astype(BF16)

    @pl.when(i % nb == 0)
    def _stream_start():
        xp_ref[0:SUBLANE, :] = conv0_ref[...]

    cw = cw_ref[...]
    prev = xp_ref[...]
    conv_cols = [slice(c0, c0 + col_chunk) for c0 in range(0, DN_QKV, col_chunk)]
    seg_cols = [(slice(c0, min(c0 + col_chunk, off + width)), epilogue)
                for off, width, epilogue in _AB_SEGMENTS for c0 in range(off, off + width, col_chunk)]

    def conv_chunk(cs, raw_full, half):
        cs = slice(cs.start + half * LANE, cs.start + (half + 1) * LANE)
        raw = raw_full[:, half * LANE:(half + 1) * LANE]
        xa = jnp.concatenate([prev[:, cs], raw], axis=0)
        y = pltpu.roll(xa, n - (SUBLANE - 3), axis=0)[0:tm] * cw[0:1, cs]
        y = y + pltpu.roll(xa, n - (SUBLANE - 2), axis=0)[0:tm] * cw[1:2, cs]
        y = y + pltpu.roll(xa, n - (SUBLANE - 1), axis=0)[0:tm] * cw[2:3, cs]
        y = y + raw * cw[3:4, cs]
        o_ref[:, cs] = _silu(y)
        last = raw[tm - SUBLANE:tm, :]
        xp_ref[:, cs] = last
        convn_ref[:, cs] = last

    def seg_chunk(cs, epilogue):
        r = _mm(xn, w_ref[:, cs])
        if epilogue == "silu":
            r = _silu(r)
        elif epilogue == "gla_q_scale":
            r = r * (GLA_DK ** -0.5)
        o_ref[:, cs] = r

    pending = []
    for j in range(max(len(conv_cols), len(seg_cols))):
        if j < len(conv_cols):
            raw = _mm(xn, w_ref[:, conv_cols[j]])
            if pending:
                conv_chunk(*pending.pop(0))
            pending += [(conv_cols[j], raw, half) for half in range(col_chunk // LANE)]
        if j < len(seg_cols):
            seg_chunk(*seg_cols[j])
            if pending:
                conv_chunk(*pending.pop(0))
    for item in pending:
        conv_chunk(*item)


def _pre_ab(x, g, w, conv0, convw, tm, rows_per_stream):
    t, d = x.shape
    nb = rows_per_stream // tm
    assert nb * tm == rows_per_stream, (rows_per_stream, tm)
    nstreams = t // rows_per_stream
    state_spec = pl.BlockSpec((None, SUBLANE, DN_QKV), lambda i: (i // nb, 0, 0))
    return pl.pallas_call(
        functools.partial(_pre_ab_kernel, nb=nb, col_chunk=256),
        grid=(t // tm,),
        in_specs=[pl.BlockSpec((tm, d), lambda i: (i, 0)),
                  _const_spec((1, d)),
                  _const_spec((d, AB_PAD)),
                  state_spec,
                  _const_spec(convw.shape)],
        out_specs=[pl.BlockSpec((tm, AB_PAD), lambda i: (i, 0)), state_spec],
        out_shape=[jax.ShapeDtypeStruct((t, AB_PAD), F32),
                   jax.ShapeDtypeStruct((nstreams, SUBLANE, DN_QKV), F32)],
        scratch_shapes=[pltpu.VMEM((SUBLANE, DN_QKV), F32)],
        compiler_params=pltpu.CompilerParams(dimension_semantics=("arbitrary",),
                                             vmem_limit_bytes=VMEM_LIMIT),
        name="pre_ab",
    )(x, g.reshape(1, d), w, conv0, convw)


def _pre_qkv_kernel(x_ref, g_ref, w_ref, o_ref, k_ref, v_ref, *, col_chunk):
    xn = _rms(x_ref[...], g_ref[...]).astype(BF16)
    dm = k_ref.shape[-1]
    for c0 in range(0, 3 * dm, col_chunk):
        c1 = c0 + col_chunk
        r = _mm(xn, w_ref[:, c0:c1])
        if c1 <= dm:
            o_ref[:, c0:c1] = (r * ATT_QSCALE).astype(BF16)
        else:
            o_ref[:, c0:c1] = r.astype(BF16)
            kv_ref = k_ref if c1 <= 2 * dm else v_ref
            kv_ref[:, c0 % dm:c0 % dm + col_chunk] = r


def _pre_qkv(x, g, w, tm, rows_per_stream):
    t, d = x.shape
    dm = w.shape[1] // 3
    nb = max(rows_per_stream // tm, 1)
    tail = pl.BlockSpec((tm, dm), lambda i: (i // nb, 0))
    return pl.pallas_call(
        functools.partial(_pre_qkv_kernel, col_chunk=512),
        grid=(t // tm,),
        in_specs=[pl.BlockSpec((tm, d), lambda i: (i, 0)),
                  _const_spec((1, d)),
                  _const_spec((d, 3 * dm))],
        out_specs=[pl.BlockSpec((tm, 3 * dm), lambda i: (i, 0)), tail, tail],
        out_shape=[jax.ShapeDtypeStruct((t, 3 * dm), BF16),
                   jax.ShapeDtypeStruct((t // nb, dm), F32),
                   jax.ShapeDtypeStruct((t // nb, dm), F32)],
        compiler_params=pltpu.CompilerParams(dimension_semantics=("arbitrary",),
                                             vmem_limit_bytes=VMEM_LIMIT),
        name="pre_qkv",
    )(x, g.reshape(1, d), w)


def _post_kernel(o_ref, x_ref, wo_ref, g_ref, wgu_ref, wd_ref, gf_ref, out_ref, acc_ref, *, ff_chunk, final):
    x1 = x_ref[...] + _mm(o_ref[...], wo_ref[...])
    xn = _rms(x1, g_ref[...]).astype(BF16)
    dff = wd_ref.shape[0]
    for f0 in range(0, dff, ff_chunk):
        gate = _mm(xn, wgu_ref[:, f0:f0 + ff_chunk])
        up = _mm(xn, wgu_ref[:, dff + f0:dff + f0 + ff_chunk])
        a = (gate * jax.nn.sigmoid(gate) * up).astype(BF16)
        part = _mm(a, wd_ref[f0:f0 + ff_chunk, :])
        if f0 == 0:
            acc_ref[...] = part
        else:
            acc_ref[...] += part
    x2 = x1 + acc_ref[...]
    if final:
        x2 = _rms(x2, gf_ref[...])
    out_ref[...] = x2


def _post(o, x, wo, g, wgu, wd, gf, tm, final):
    t, d = x.shape
    dm = o.shape[1]
    dff = wd.shape[0]
    return pl.pallas_call(
        functools.partial(_post_kernel, ff_chunk=256, final=final),
        grid=(t // tm,),
        in_specs=[pl.BlockSpec((tm, dm), lambda i: (i, 0)),
                  pl.BlockSpec((tm, d), lambda i: (i, 0)),
                  _const_spec((dm, d)),
                  _const_spec((1, d)),
                  _const_spec((d, 2 * dff)),
                  _const_spec((dff, d)),
                  _const_spec((1, d))],
        out_specs=pl.BlockSpec((tm, d), lambda i: (i, 0)),
        out_shape=jax.ShapeDtypeStruct((t, d), F32),
        scratch_shapes=[pltpu.VMEM((tm, d), F32)],
        compiler_params=pltpu.CompilerParams(dimension_semantics=("parallel",),
                                             vmem_limit_bytes=VMEM_LIMIT),
        name="post_ffn",
    )(o, x, wo, g.reshape(1, d), wgu, wd, gf.reshape(1, d))


def _score_group_of_fg(fg, c):
    return (2 * fg * c) // LANE


@functools.lru_cache(maxsize=None)
def _gla_pairs(c):
    hs = LANE // c
    pairs = []
    for sg in range(H * c // LANE):
        for qd in range(GW // LANE):
            if set(range(sg * hs, (sg + 1) * hs)) & set(range(4 * qd, 4 * qd + 4)):
                pairs.append((sg, qd))
    return tuple(pairs)


@functools.lru_cache(maxsize=None)
def _ab_consts(c):
    w = H * c
    hs = LANE // c
    nsg = w // LANE
    nsb = c // GLA_SUB
    i = np.arange(c)
    r = np.arange(LANE)
    w_pos = np.arange(w) % c
    ltri = (i[None, :] <= i[:, None])
    eye_w = (i[:, None] == w_pos[None, :])
    causal_w = (w_pos[None, :] <= i[:, None])
    strict_w = (w_pos[None, :] < i[:, None])
    dmask = causal_w & ((i[:, None] // GLA_SUB) == (w_pos[None, :] // GLA_SUB))
    row_hl = r // c
    row_j = r % c
    m_ss = (row_hl[:, None] == row_hl[None, :])
    m_ff = ((r // DN_DV)[:, None] == (r // DN_DV)[None, :])
    m_sf = np.stack([(_score_group_of_fg(fg, c) * hs + row_hl)[:, None] == (2 * fg + r // DN_DV)[None, :]
                     for fg in range(NFG)])
    mg = np.stack([(sg * hs + row_hl)[:, None] == (r // 16)[None, :] for sg in range(nsg)])
    pairs = _gla_pairs(c)
    m_sq = np.stack([(sg * hs + row_hl)[:, None] == (4 * qd + r // GLA_DK)[None, :] for sg, qd in pairs])
    m_sq3 = np.stack([np.concatenate([m & ((row_j // GLA_SUB) == j)[:, None] for j in range(max(nsb - 1, 1))],
                                     axis=1) for m in m_sq])
    m_fq = np.stack([(2 * fg + r // DN_DV)[:, None] == (4 * (fg // 2) + r // GLA_DK)[None, :]
                     for fg in range(NFG)])
    sel = np.zeros((6, LANE, LANE), np.float32)
    cst = np.zeros((SUBLANE, LANE), np.float32)
    for h in range(H):
        for p in range(3):
            sel[p, h, 16 * h + p] = 1.0
            sel[3 + p, h, 16 * h + 3 + p] = -1.0
            cst[0, 16 * h + 3 + p] = 1.0
            cst[1, 16 * h + p] = 1.0
    expg = np.zeros((LANE, HW), np.float32)
    expb = np.zeros((LANE, HW), np.float32)
    for h in range(H):
        expg[h, DN_DV * h:DN_DV * (h + 1)] = 1.0
        expb[H + h, DN_DV * h:DN_DV * (h + 1)] = 1.0
    f = lambda a: jnp.asarray(a, F32)
    b = lambda a: jnp.asarray(a, BF16)
    return dict(ltri=b(ltri), eye_w=f(eye_w), causal_w=f(causal_w), strict_w=f(strict_w), dmask=f(dmask),
                m_ss=b(m_ss), m_sf=b(m_sf), m_ffb=b(m_ff), m_ff=f(m_ff), mg=b(mg), m_sq=b(m_sq), m_sq3=b(m_sq3),
                m_fq=f(m_fq), sel=b(sel), cst=f(cst), expg=b(expg), expb=b(expb))


_AB_CONST_ORDER = ("ltri", "eye_w", "causal_w", "strict_w", "dmask", "m_ss", "m_sf", "m_ffb", "m_ff", "mg",
                   "m_sq", "m_sq3", "m_fq", "sel", "cst", "expg", "expb")


def _headsum(xs, bones):
    rows = xs[0].shape[0]
    pieces = [_grp(p, g) for x in xs for p in _split(x, 2) for g in range(NFG)]
    half = len(pieces) // 2
    y = jnp.concatenate(_group_mm([jnp.concatenate(pieces[:half], axis=0), jnp.concatenate(pieces[half:], axis=0)],
                                 [bones, bones]), axis=0)
    outs = []
    for n in range(len(xs)):
        base = 2 * NFG * n
        outs.append(jnp.concatenate(
            [y[(base + g) * rows:(base + g + 1) * rows] + y[(base + NFG + g) * rows:(base + NFG + g + 1) * rows]
             for g in range(NFG)], axis=1))
    return outs


def _ab_kernel(proj_ref, dn0_ref, gl0_ref, gparam_ref, w2_ref, gkb_ref, onorm_ref,
               ltri_ref, eye_ref, causal_ref, strict_ref, dmask_ref, mss_ref, msf_ref, mffb_ref, mff_ref, mg_ref,
               msq_ref, msq3_ref, mfq_ref, sel_ref, cst_ref, expg_ref, expb_ref,
               o_ref, dns_ref, gls_ref,
               sdn_ref, sgl_ref, *, c, nc, ns):
    i = pl.program_id(1)
    nsb = c // GLA_SUB
    hs = LANE // c
    nsg = H * c // LANE
    sg_of = [_score_group_of_fg(fg, c) for fg in range(NFG)]
    streams = range(ns)

    @pl.when(i == 0)
    def _load_state():
        sdn_ref[...] = dn0_ref[...]
        sgl_ref[...] = gl0_ref[...]

    bones = mffb_ref[...]
    mss = mss_ref[...]
    ltri = ltri_ref[...]
    gp = gparam_ref[...]
    onorm = onorm_ref[...]

    q = [proj_ref[s, :, OFF_QKV:OFF_QKV + HW] for s in streams]
    k = [proj_ref[s, :, OFF_QKV + HW:OFF_QKV + 2 * HW] for s in streams]
    v = [proj_ref[s, :, OFF_QKV + 2 * HW:OFF_QKV + 3 * HW] for s in streams]

    for s in streams:
        sq, sk = _headsum([q[s] * q[s], k[s] * k[s]], bones)
        q[s] = q[s] * lax.rsqrt(sq + EPS) * (DN_DK ** -0.5)
        k[s] = k[s] * lax.rsqrt(sk + EPS)

    pairs = _gla_pairs(c)
    small = [proj_ref[s, :, OFF_SMALL:OFF_SMALL + LANE] for s in streams]
    gl = {}

    def gla_gate_logits():
        gl["z"] = [_mm(small[s].astype(BF16), w2_ref[...]) + gkb_ref[...] for s in streams]

    def gla_cumsum():
        gl["bcum"] = [_mmx_left(ltri, -_softplus(-gl["z"][s]) * (1.0 / GLA_NORMALIZER), 3) for s in streams]

    def gla_scores_and_state():
        gl["a"], gl["o_inter"], gl["v2b"] = [], [], []
        for s in streams:
            bcum = gl["bcum"][s]
            q2 = proj_ref[s, :, OFF_GQ:OFF_GQ + GW]
            k2 = proj_ref[s, :, OFF_GK:OFF_GK + GW]
            v2b = proj_ref[s, :, OFF_GV:OFF_GV + HW].astype(BF16)
            blast = bcum[c - 1:c, :]
            qe = (q2 * jnp.exp(bcum)).astype(BF16)
            kdec = (k2 * jnp.exp(blast - bcum)).astype(BF16)
            eblast = jnp.exp(blast)
            rmid = jnp.concatenate(
                [jnp.broadcast_to(bcum[GLA_SUB * t + GLA_SUB // 2:GLA_SUB * t + GLA_SUB // 2 + 1, :], (GLA_SUB, GW))
                 for t in range(nsb)], axis=0)
            qm = (q2 * jnp.exp(bcum - rmid)).astype(BF16)
            km = (k2 * jnp.exp(rmid - bcum)).astype(BF16)
            parts = [None] * nsg
            prods = _group_mm([_grp(qm, qd) for _, qd in pairs],
                             [_tile_rows(_grp(km, qd), hs) * msq_ref[n] for n, (_, qd) in enumerate(pairs)], nt=True)
            for n, (sg, _) in enumerate(pairs):
                parts[sg] = prods[n] if parts[sg] is None else parts[sg] + prods[n]
            a_gl = dmask_ref[...] * jnp.concatenate(parts, axis=1)
            if nsb > 1:
                rend = jnp.concatenate(
                    [jnp.broadcast_to(bcum[GLA_SUB * (t + 1) - 1:GLA_SUB * (t + 1), :], (GLA_SUB, GW))
                     for t in range(nsb)], axis=0)
                kr = (k2 * jnp.exp(rend - bcum)).astype(BF16)
                rowi = lax.broadcasted_iota(jnp.int32, (c, GW), 0)
                qs = []
                for t in range(nsb - 1):
                    e = jnp.where(rowi >= GLA_SUB * (t + 1), bcum - rend[GLA_SUB * t:GLA_SUB * t + 1, :], -jnp.inf)
                    qs.append((q2 * jnp.exp(e)).astype(BF16))
                parts = [None] * nsg
                prods = _group_mm(
                    [jnp.concatenate([_grp(x, qd) for x in qs], axis=1) for _, qd in pairs],
                    [_tile_rows(jnp.concatenate([_grp(kr, qd)] * (nsb - 1), axis=1), hs) * msq3_ref[n]
                     for n, (_, qd) in enumerate(pairs)], nt=True)
                for n, (sg, _) in enumerate(pairs):
                    parts[sg] = prods[n] if parts[sg] is None else parts[sg] + prods[n]
                a_gl = a_gl + jnp.concatenate(parts, axis=1)
            st = [sgl_ref[s, fg] for fg in range(NFG)]
            o_inter = _group_mm([_grp(qe, fg // 2) for fg in range(NFG)], [x.astype(BF16) for x in st], nt=True)
            for fg in range(NFG):
                qd = fg // 2
                sgl_ref[s, fg] = st[fg] * _grp(eblast, qd) + _mm_tn(_grp(v2b, fg), _grp(kdec, qd)) * mfq_ref[fg]
            gl["a"].append(a_gl.astype(BF16))
            gl["o_inter"].append(o_inter)
            gl["v2b"].append(v2b)

    def gla_out():
        gl["o"] = []
        for s in streams:
            intra = _group_mm([_grp(gl["a"][s], sg_of[fg]) for fg in range(NFG)],
                             [_tile_rows(_grp(gl["v2b"][s], fg), hs) * msf_ref[fg] for fg in range(NFG)])
            gl["o"].append(jnp.concatenate([gl["o_inter"][s][fg] + intra[fg] for fg in range(NFG)], axis=1))

    gla_gate_logits()
    gcum_parts = []
    for s in streams:
        g_full = -jnp.exp(gp[0:1, :]) * _softplus(small[s] + gp[1:2, :])
        gcum_parts.append(_split(_mmx_left(ltri, g_full, 3), 3))
    gla_cumsum()
    beta_w, gcum_w, g1, g2 = [], [], [], []
    for s in streams:
        gw, a1, a2 = None, cst_ref[0:1, :], cst_ref[1:2, :]
        for p in range(3):
            t = _mm(gcum_parts[s][p], expg_ref[...])
            gw = t if gw is None else gw + t
            a1 = a1 + _mm(gcum_parts[s][p], sel_ref[p])
            a2 = a2 + _mm(gcum_parts[s][p], sel_ref[3 + p])
        gcum_w.append(gw)
        g1.append(a1.astype(BF16))
        g2.append(a2.astype(BF16))
        beta_w.append(_mmx(jax.nn.sigmoid(small[s]), expb_ref[...], 2))
    eg_w = [jnp.exp(gcum_w[s]) for s in streams]
    edec_w = [jnp.exp(gcum_w[s][c - 1:c, :] - gcum_w[s]) for s in streams]
    eglast_w = [jnp.exp(gcum_w[s][c - 1:c, :]) for s in streams]
    gla_scores_and_state()

    decay_w = []
    for s in streams:
        d_w = jnp.concatenate(_group_mm([g1[s]] * nsg, [_tile_rows(g2[s], hs) * mg_ref[sg] for sg in range(nsg)],
                                       nt=True), axis=1)
        decay_w.append(jnp.exp(jnp.where(causal_ref[...] > 0.0, d_w, -jnp.inf)))

    kb = [k[s] * beta_w[s] for s in streams]
    a_w, qkd_b = [], []
    for s in streams:
        kq = jnp.concatenate([kb[s], q[s]], axis=0).astype(BF16)
        kbf = k[s].astype(BF16)
        parts = [None] * nsg
        prods = _group_mm([_grp(kq, fg) for fg in range(NFG)],
                         [_tile_rows(_grp(kbf, fg), hs) * msf_ref[fg] for fg in range(NFG)], nt=True)
        for fg in range(NFG):
            parts[sg_of[fg]] = prods[fg] if parts[sg_of[fg]] is None else parts[sg_of[fg]] + prods[fg]
        r = jnp.concatenate(parts, axis=1)
        a_w.append(strict_ref[...] * r[0:c] * decay_w[s])
        qkd_b.append((r[c:2 * c] * decay_w[s]).astype(BF16))

    def ssprod(x_w, y_w):
        xb = x_w.astype(BF16)
        yb = y_w.astype(BF16)
        return jnp.concatenate(_group_mm([_grp(xb, sg) for sg in range(nsg)],
                                        [_tile_rows(_grp(yb, sg), hs) * mss for sg in range(nsg)]), axis=1)

    nlev = int(math.log2(c))
    pw = [-a_w[s] for s in streams]
    t_w = [eye_ref[...] + pw[s] for s in streams]
    pw = [ssprod(pw[s], pw[s]) for s in streams]
    gla_out()
    for _ in range(2, nlev):
        rr = [ssprod(jnp.concatenate([pw[s], t_w[s]], axis=0), pw[s]) for s in streams]
        t_w = [t_w[s] + rr[s][c:2 * c] for s in streams]
        pw = [rr[s][0:c] for s in streams]
    rr = [ssprod(t_w[s], pw[s]) for s in streams]
    tb = [(t_w[s] + rr[s]).astype(BF16) for s in streams]

    vbb = [(v[s] * beta_w[s]).astype(BF16) for s in streams]
    kbe = [(kb[s] * eg_w[s]).astype(BF16) for s in streams]
    qeg = [q[s] * eg_w[s] for s in streams]
    kdc = [(k[s] * edec_w[s]).astype(BF16) for s in streams]
    fgs = range(NFG)
    uw, rs, s_dn, vnb = {}, {}, {}, {}
    for s in streams:
        uw[s] = _group_mm([_grp(tb[s], sg_of[fg]) for fg in fgs],
                         [jnp.concatenate([_tile_rows(_grp(vbb[s], fg), hs) * msf_ref[fg],
                                           _tile_rows(_grp(kbe[s], fg), hs) * msf_ref[fg]], axis=1) for fg in fgs])
    for s in streams:
        s_dn[s] = [sdn_ref[s, fg] for fg in fgs]
        rs[s] = _group_mm([jnp.concatenate([uw[s][fg][:, LANE:2 * LANE], _grp(qeg[s], fg)], axis=0).astype(BF16)
                          for fg in fgs], [s_dn[s][fg].astype(BF16) for fg in fgs])
    o_dn = []
    for s in streams:
        vnb[s] = [(uw[s][fg][:, 0:LANE] - rs[s][fg][0:c]).astype(BF16) for fg in fgs]
        intra = _group_mm([_grp(qkd_b[s], sg_of[fg]) for fg in fgs],
                         [_tile_rows(vnb[s][fg], hs) * msf_ref[fg] for fg in fgs])
        o_dn.append(jnp.concatenate([rs[s][fg][c:2 * c] + intra[fg] for fg in fgs], axis=1))
    for s in streams:
        for fg in fgs:
            sdn_ref[s, fg] = (s_dn[s][fg] * _grp(eglast_w[s], fg)
                              + _mm_tn(_grp(kdc[s], fg), vnb[s][fg]) * mff_ref[...])
    o_gl = gl["o"]

    for s in streams:
        md, mgl = _headsum([o_dn[s] * o_dn[s], o_gl[s] * o_gl[s]], bones)
        gd = proj_ref[s, :, OFF_DGATE:OFF_DGATE + HW]
        gg = proj_ref[s, :, OFF_GGATE:OFF_GGATE + HW]
        y_dn = (o_dn[s] * lax.rsqrt(md * (1.0 / DN_DV) + EPS)) * onorm[0:1, :] * gd
        y_gl = (o_gl[s] * lax.rsqrt(mgl * (1.0 / GLA_DV) + EPS)) * onorm[1:2, :] * gg
        o_ref[s, :, 0:HW] = y_dn.astype(BF16)
        o_ref[s, :, HW:2 * HW] = y_gl.astype(BF16)

    @pl.when(i == nc - 1)
    def _store_state():
        dns_ref[...] = sdn_ref[...]
        gls_ref[...] = sgl_ref[...]


def _ab_mixer(proj, dn0, gl0, gparam, w2p, gkb, onorm, c):
    b, l, _ = proj.shape
    nc = l // c
    ns = AB_STREAMS
    consts = _ab_consts(c)
    cvals = [consts[n] for n in _AB_CONST_ORDER]
    per_b = lambda shape: pl.BlockSpec((ns,) + shape, lambda bi, ci: (bi,) + (0,) * len(shape))
    state = (NFG, LANE, LANE)
    in_specs = ([pl.BlockSpec((ns, c, AB_PAD), lambda bi, ci: (bi, ci, 0)),
                 per_b(state), per_b(state),
                 _const_spec(gparam.shape), _const_spec(w2p.shape),
                 _const_spec(gkb.shape), _const_spec(onorm.shape)]
                + [_const_spec(a.shape) for a in cvals])
    out_specs = [pl.BlockSpec((ns, c, 2 * HW), lambda bi, ci: (bi, ci, 0)),
                 per_b(state), per_b(state)]
    out_shape = [jax.ShapeDtypeStruct((b, l, 2 * HW), BF16),
                 jax.ShapeDtypeStruct((b,) + state, F32),
                 jax.ShapeDtypeStruct((b,) + state, F32)]
    return pl.pallas_call(
        functools.partial(_ab_kernel, c=c, nc=nc, ns=ns),
        grid=(b // ns, nc),
        in_specs=in_specs,
        out_specs=out_specs,
        out_shape=out_shape,
        scratch_shapes=[pltpu.VMEM((ns,) + state, F32),
                        pltpu.VMEM((ns,) + state, F32)],
        compiler_params=pltpu.CompilerParams(dimension_semantics=("parallel", "arbitrary"),
                                             vmem_limit_bytes=VMEM_LIMIT),
        name="ab_mixer",
    )(proj, dn0, gl0, gparam, w2p, gkb, onorm, *cvals)


_DN_PLACE = np.eye(2, dtype=np.float32)
_GLA_PLACE = np.zeros((2, 2, 4), np.float32)
for _fp in range(2):
    for _hl in range(2):
        _GLA_PLACE[_fp, _hl, 2 * _fp + _hl] = 1.0


def _dn_state_to_groups(s):
    b = s.shape[0]
    s5 = s.reshape(b, NFG, 2, DN_DK, 1, DN_DV)
    return (s5 * _DN_PLACE[None, None, :, None, :, None]).reshape(b, NFG, LANE, LANE)


def _dn_state_from_groups(g):
    b = g.shape[0]
    g6 = g.reshape(b, NFG, 2, DN_DK, 2, DN_DV)
    return jnp.sum(g6 * _DN_PLACE[None, None, :, None, :, None], axis=4).reshape(b, DN_HEADS, DN_DK, DN_DV)


def _gla_state_to_groups(s):
    b = s.shape[0]
    st = jnp.swapaxes(s, -1, -2).reshape(b, 2, 2, 2, GLA_DV, 1, GLA_DK)
    return (st * _GLA_PLACE[None, None, :, :, None, :, None]).reshape(b, NFG, LANE, LANE)


def _gla_state_from_groups(g):
    b = g.shape[0]
    g7 = g.reshape(b, 2, 2, 2, GLA_DV, 4, GLA_DK)
    st = jnp.sum(g7 * _GLA_PLACE[None, None, :, :, None, :, None], axis=5)
    return jnp.swapaxes(st.reshape(b, GLA_HEADS, GLA_DV, GLA_DK), -1, -2)


def _softmax2_rows(parts):
    m = None
    for s in parts:
        t = jnp.max(s, axis=-1, keepdims=True)
        m = t if m is None else jnp.maximum(m, t)
    es = [jnp.exp2(s - m) for s in parts]
    l = None
    for e in es:
        t = jnp.sum(e, axis=-1, keepdims=True)
        l = t if l is None else l + t
    inv = 1.0 / l
    return [e * inv for e in es]


def _fold_lanes(x, op):
    out = x[:, 0:LANE]
    for c0 in range(LANE, x.shape[1], LANE):
        out = op(out, x[:, c0:c0 + LANE])
    return out


def _attn_prompt_kernel(q_ref, k0_ref, k1_ref, k2_ref, v0_ref, v1_ref, v2_ref, bias_ref, o_ref, *, qb):
    i = pl.program_id(1)
    offs = (jnp.where(i >= 2, 0.0, -jnp.inf), jnp.where(i >= 1, 0.0, -jnp.inf), 0.0)
    k_refs = (k0_ref, k1_ref, k2_ref)
    v_refs = (v0_ref, v1_ref, v2_ref)

    nq = qb // CHUNK
    tpb = qb // LANE
    cpt = LANE // CHUNK

    def tile_in_band(qc, lt):
        return cpt * lt + cpt - 1 >= qc + 2 * nq - C_BAND_CHUNKS and cpt * lt <= qc + 2 * nq

    def scores(h):
        sl = slice(C_HD * h, C_HD * (h + 1))
        qh = q_ref[:, sl]
        return [_mm_nt(qh, k_refs[t][:, sl]) for t in range(3)]

    def finish(h, ss):
        sl = slice(C_HD * h, C_HD * (h + 1))
        s_t, ems = {}, []
        for qc in range(nq):
            rows = slice(CHUNK * qc, CHUNK * (qc + 1))
            em = None
            for t in range(3):
                et = None
                for j in range(tpb):
                    lt = t * tpb + j
                    if tile_in_band(qc, lt):
                        s = ss[t][rows, LANE * j:LANE * (j + 1)] + bias_ref[h, rows, LANE * lt:LANE * (lt + 1)]
                        s_t[qc, lt] = s
                        et = s if et is None else jnp.maximum(et, s)
                if et is not None:
                    et = et + offs[t]
                    em = et if em is None else jnp.maximum(em, et)
            ems.append(em)
        m = jnp.max(jnp.concatenate(ems, axis=0), axis=-1, keepdims=True)
        shift = [m - offs[t] for t in range(3)]
        p_tiles, els = [], []
        for qc in range(nq):
            rows = slice(CHUNK * qc, CHUNK * (qc + 1))
            el, row_tiles = None, []
            for lt in range(3 * tpb):
                if (qc, lt) in s_t:
                    p = jnp.exp2(s_t[qc, lt] - shift[lt // tpb][rows])
                    el = p if el is None else el + p
                    row_tiles.append(p.astype(BF16))
                else:
                    row_tiles.append(jnp.zeros((CHUNK, LANE), BF16))
            els.append(el)
            p_tiles.append(row_tiles)
        inv_l = [1.0 / jnp.sum(jnp.concatenate(els, axis=0), axis=-1, keepdims=True)]
        acc = None
        for t in range(3):
            p_blk = jnp.concatenate(
                [jnp.concatenate(p_tiles[qc][t * tpb:(t + 1) * tpb], axis=1) for qc in range(nq)], axis=0)
            at = _mm(p_blk, v_refs[t][:, sl])
            acc = at if acc is None else acc + at
        o_ref[:, sl] = (acc * jnp.concatenate(inv_l, axis=0)).astype(BF16)

    ss_next = scores(0)
    for h in range(C_HEADS):
        ss = ss_next
        if h + 1 < C_HEADS:
            ss_next = scores(h + 1)
        finish(h, ss)


def _attn_prompt(qkv, bias, qb):
    b, l, _ = qkv.shape
    dm = C_HEADS * C_HD
    kv_spec = lambda colblk, back: pl.BlockSpec(
        (None, qb, dm), lambda bi, i: (bi, jnp.maximum(i - back, 0), colblk))
    return pl.pallas_call(
        functools.partial(_attn_prompt_kernel, qb=qb),
        grid=(b, l // qb),
        in_specs=[pl.BlockSpec((None, qb, dm), lambda bi, i: (bi, i, 0)),
                  kv_spec(1, 2), kv_spec(1, 1), kv_spec(1, 0),
                  kv_spec(2, 2), kv_spec(2, 1), kv_spec(2, 0),
                  _const_spec(bias.shape)],
        out_specs=pl.BlockSpec((None, qb, dm), lambda bi, i: (bi, i, 0)),
        out_shape=jax.ShapeDtypeStruct((b, l, dm), BF16),
        compiler_params=pltpu.CompilerParams(dimension_semantics=("parallel", "parallel"),
                                             vmem_limit_bytes=VMEM_LIMIT),
        name="attn_prompt",
    )(qkv, qkv, qkv, qkv, qkv, qkv, qkv, bias)


def _attn_sample_kernel(q_ref, k_ref, v_ref, ck_ref, cv_ref, bc_ref, bn_ref, o_ref):
    sls = [slice(C_HD * h, C_HD * (h + 1)) for h in range(C_HEADS)]
    scores = [(_mm_nt(q_ref[:, sl], ck_ref[:, sl].astype(BF16)) + bc_ref[h],
               _mm_nt(q_ref[:, sl], k_ref[:, sl]) + bn_ref[h]) for h, sl in enumerate(sls)]
    probs = [_softmax2_rows(list(s)) for s in scores]
    for (pc, pn), sl in zip(probs, sls):
        oh = _mm(pc.astype(BF16), cv_ref[:, sl].astype(BF16)) + _mm(pn.astype(BF16), v_ref[:, sl])
        o_ref[:, sl] = oh.astype(BF16)


def _attn_sample(qkv, ck, cv, bias_c, bias_n):
    b, l, _ = qkv.shape
    dm = C_HEADS * C_HD
    ncache = ck.shape[1]
    new_spec = lambda colblk: pl.BlockSpec((None, l, dm), lambda bi: (bi, 0, colblk))
    cache_spec = pl.BlockSpec((None, ncache, dm), lambda bi: (bi, 0, 0))
    return pl.pallas_call(
        _attn_sample_kernel,
        grid=(b,),
        in_specs=[new_spec(0), new_spec(1), new_spec(2), cache_spec, cache_spec,
                  _const_spec(bias_c.shape), _const_spec(bias_n.shape)],
        out_specs=pl.BlockSpec((None, l, dm), lambda bi: (bi, 0, 0)),
        out_shape=jax.ShapeDtypeStruct((b, l, dm), BF16),
        compiler_params=pltpu.CompilerParams(dimension_semantics=("parallel",),
                                             vmem_limit_bytes=VMEM_LIMIT),
        name="attn_sample",
    )(qkv, qkv, qkv, ck, cv, bias_c, bias_n)


def _prompt_bias(table, qb):
    nh = table.shape[0]
    rows, cols = qb, 3 * qb
    period = rows + cols
    far_past = jnp.broadcast_to(table[:, 2 * MAX_REL:], (nh, 3 * qb - 1 - MAX_REL))
    far_future = jnp.broadcast_to(table[:, :1], (nh, qb - 1 - MAX_REL))
    w = jnp.concatenate([far_past, table[:, ::-1], far_future, jnp.zeros((nh, 1), table.dtype)], axis=1)
    w = jnp.roll(w, -(rows - 1), axis=1)
    bias = jnp.tile(w, (1, rows))[:, :rows * (period - 1)].reshape(nh, rows, period - 1)[:, :, :cols]
    rq = np.arange(rows)[:, None]
    rk = np.arange(cols)[None, :]
    qc = rq // CHUNK
    kc = rk // CHUNK - (2 * qb) // CHUNK
    band = (kc <= qc) & (kc >= qc - C_BAND_CHUNKS)
    return jnp.where(jnp.asarray(band)[None], bias.astype(F32) * LOG2E, -jnp.inf)


def _sample_bias(table, l, ncache):
    qpos = PAST_LEN + np.arange(l)
    kpos = np.concatenate([PAST_LEN - ncache + np.arange(ncache), PAST_LEN + np.arange(l)])
    rel = np.clip(qpos[:, None] - kpos[None, :], -MAX_REL, MAX_REL) + MAX_REL
    qch = qpos // CHUNK
    kch = kpos // CHUNK
    valid = (kch[None, :] <= qch[:, None]) & (kch[None, :] >= qch[:, None] - C_BAND_CHUNKS)
    bias = jnp.where(jnp.asarray(valid)[None], table[:, rel].astype(F32) * LOG2E, -jnp.inf)
    return bias[:, :, :ncache], bias[:, :, ncache:]


def _permute_w_in(w):
    d = w.shape[0]
    o_a = DN_QKV
    o_b = o_a + DN_HEADS
    o_gate = o_b + DN_HEADS
    o_gq = o_gate + HW
    o_gk = o_gq + GW
    o_gv = o_gk + GW
    o_lr = o_gv + HW
    o_gg = o_lr + GLA_RANK
    return jnp.concatenate(
        [w[:, :DN_QKV], w[:, o_gate:o_gq], w[:, o_gq:o_gk], w[:, o_gk:o_gv], w[:, o_gv:o_lr],
         w[:, o_gg:o_gg + HW], w[:, o_a:o_gate], w[:, o_lr:o_gg],
         jnp.zeros((d, LANE - 2 * DN_HEADS - GLA_RANK), w.dtype)], axis=1)


def _trunk(x, conv0, dn0, gla0, ck, cv, w, prompt, tm):
    b, l, d = x.shape
    depth = w["ffn_norm"].shape[0]
    c = min(CHUNK, l)
    xf = x.reshape(b * l, d)
    convs, dns, glas, ks, vs = [], [], [], [], []
    for layer in range(depth):
        i = layer // 2
        if layer % 2 == 0:
            if conv0 is None:
                conv0p = jnp.zeros((b, SUBLANE, DN_QKV), F32)
                dn0g = gl0g = jnp.zeros((b, NFG, LANE, LANE), F32)
            else:
                conv0p = jnp.pad(conv0[i], ((0, 0), (SUBLANE - (CONV_W - 1), 0), (0, 0)))
                dn0g, gl0g = _dn_state_to_groups(dn0[i]), _gla_state_to_groups(gla0[i])
            proj, convn = _pre_ab(xf, w["ab_norm"][i], w["ab_w_in"][i], conv0p, w["dn_conv_w"][i], min(tm, l), l)
            o, dn_s, gl_s = _ab_mixer(proj.reshape(b, l, AB_PAD), dn0g, gl0g, w["gparam"][i],
                                      w["gk_w2p"][i], w["gla_gk_b"][i], w["onorm"][i], c)
            convs.append(convn[:, SUBLANE - (CONV_W - 1):, :])
            dns.append(_dn_state_from_groups(dn_s))
            glas.append(_gla_state_from_groups(gl_s))
            wo = w["ab_w_out"][i]
        else:
            dm = C_HEADS * C_HD
            rows = min(C_PAST, l) if prompt else l
            assert l <= tm or rows == tm, (l, tm, rows)
            qkv, k_rows, v_rows = _pre_qkv(xf, w["c_norm"][i], w["c_w_qkv"][i], tm, l)
            qkv = qkv.reshape(b, l, 3 * dm)
            ks.append(k_rows.reshape(b, -1, C_HEADS, C_HD)[:, -rows:])
            vs.append(v_rows.reshape(b, -1, C_HEADS, C_HD)[:, -rows:])
            if prompt:
                o = _attn_prompt(qkv, _prompt_bias(w["c_rel_bias"][i], ATT_QB), ATT_QB)
            else:
                ncache = ck.shape[2]
                bias_c, bias_n = _sample_bias(w["c_rel_bias"][i], l, ncache)
                o = _attn_sample(qkv, ck[i].reshape(b, ncache, dm), cv[i].reshape(b, ncache, dm), bias_c, bias_n)
            wo = w["c_w_out"][i]
        xf = _post(o.reshape(b * l, -1), xf, wo, w["ffn_norm"][layer], w["ffn_w_gu"][layer],
                   w["ffn_w_down"][layer], w["final_norm"], tm, final=(layer == depth - 1))
    return (xf.reshape(b, l, d), jnp.stack(convs), jnp.stack(dns), jnp.stack(glas), jnp.stack(ks), jnp.stack(vs))


def kernel(x_prompt, x_sample, state_dn_conv, state_dn, state_gla, cache_c_k, cache_c_v, ab_norm, ab_w_in, dn_conv_w, dn_a_log, dn_dt_bias, dn_out_norm, gla_gk_w2, gla_gk_b, gla_out_norm, ab_w_out, c_norm, c_w_qkv, c_rel_bias, c_w_out, ffn_norm, ffn_w_gu, ffn_w_down, final_norm):
    n_ab = ab_w_in.shape[0]
    gparam = jnp.zeros((n_ab, SUBLANE, LANE), F32)
    gparam = gparam.at[:, 0, :DN_HEADS].set(dn_a_log).at[:, 1, :DN_HEADS].set(dn_dt_bias)
    gk_w2p = jnp.zeros((n_ab, LANE, GW), F32).at[:, 2 * DN_HEADS:2 * DN_HEADS + GLA_RANK, :].set(gla_gk_w2)
    onorm = jnp.stack([jnp.tile(dn_out_norm, (1, DN_HEADS)), jnp.tile(gla_out_norm, (1, GLA_HEADS))], axis=1)
    w = {
        "ab_norm": ab_norm, "c_norm": c_norm, "ffn_norm": ffn_norm, "final_norm": final_norm,
        "ab_w_in": jnp.stack([_permute_w_in(ab_w_in[i]) for i in range(n_ab)]).astype(BF16),
        "dn_conv_w": dn_conv_w, "gparam": gparam, "gk_w2p": gk_w2p.astype(BF16),
        "gla_gk_b": gla_gk_b[:, None, :], "onorm": onorm,
        "ab_w_out": ab_w_out.astype(BF16), "c_w_qkv": c_w_qkv.astype(BF16), "c_rel_bias": c_rel_bias,
        "c_w_out": c_w_out.astype(BF16), "ffn_w_gu": ffn_w_gu.astype(BF16), "ffn_w_down": ffn_w_down.astype(BF16),
    }
    y_p, conv_p, dn_p, gla_p, ck_p, cv_p = _trunk(x_prompt, None, None, None, None, None, w, True, 512)
    ts = x_sample.shape[0] * x_sample.shape[1]
    y_s, conv_s, dn_s, gla_s, ck_s, cv_s = _trunk(x_sample, state_dn_conv, state_dn, state_gla,
                                                  cache_c_k, cache_c_v, w, False, ts)
    return (y_p, y_s, conv_p, conv_s, dn_p, dn_s, gla_p, gla_s, ck_p, ck_s, cv_p, cv_s)
```

```python
import functools
import math

import numpy as np
import jax
import jax.numpy as jnp
from jax import lax
from jax.experimental import pallas as pl
from jax.experimental.pallas import tpu as pltpu

F32 = jnp.float32
BF16 = jnp.bfloat16

EPS = 1e-6
CHUNK = 64
PAST_LEN = 2048
D_MODEL = 1024
DN_HEADS = 8
DN_DK = 64
DN_DV = 64
CONV_W = 4
DN_QKV = DN_HEADS * (2 * DN_DK + DN_DV)
GLA_HEADS = 8
GLA_DK = 32
GLA_DV = 64
GLA_RANK = 16
GLA_NORMALIZER = 16.0
GLA_SUB = 16
C_HEADS = 16
C_HD = 64
C_BAND_CHUNKS = 8
C_PAST = C_BAND_CHUNKS * CHUNK
MAX_REL = 128
ATT_QB = 256
LOG2E = math.log2(math.e)
ATT_QSCALE = C_HD ** -0.5 * LOG2E

H = DN_HEADS
HW = DN_HEADS * DN_DV
GW = GLA_HEADS * GLA_DK
LANE = 128
SUBLANE = 8
NFG = HW // LANE
AB_STREAMS = 4

OFF_QKV = 0
OFF_DGATE = DN_QKV
OFF_GQ = OFF_DGATE + HW
OFF_GK = OFF_GQ + GW
OFF_GV = OFF_GK + GW
OFF_GGATE = OFF_GV + HW
OFF_SMALL = OFF_GGATE + HW
AB_PAD = OFF_SMALL + LANE

VMEM_LIMIT = 56 * 1024 * 1024


def _mm(a, b):
    return jnp.dot(a, b, preferred_element_type=F32)


def _mm_nt(a, b):
    return lax.dot_general(a, b, (((1,), (1,)), ((), ())), preferred_element_type=F32)


def _mm_tn(a, b):
    return lax.dot_general(a, b, (((0,), (0,)), ((), ())), preferred_element_type=F32)


def _split(x, n):
    parts = []
    r = x
    for t in range(n):
        p = r.astype(BF16)
        parts.append(p)
        if t + 1 < n:
            r = r - p.astype(F32)
    return parts


def _mmx(x, m, n):
    out = None
    for p in _split(x, n):
        t = _mm(p, m)
        out = t if out is None else out + t
    return out


def _mmx_left(m, x, n):
    out = None
    for p in _split(x, n):
        t = _mm(m, p)
        out = t if out is None else out + t
    return out


def _softplus(x):
    return jnp.maximum(x, 0.0) + jnp.log(1.0 + jnp.exp(-jnp.abs(x)))


def _tile_rows(x, n):
    return jnp.concatenate([x] * n, axis=0)


def _grp(x, g):
    return x[:, LANE * g:LANE * (g + 1)]


def _group_mm(lhs, rhs, nt=False):
    return [_mm_nt(a, b) if nt else _mm(a, b) for a, b in zip(lhs, rhs)]


def _rms(x, g):
    ms = jnp.mean(x * x, axis=-1, keepdims=True)
    return (x * lax.rsqrt(ms + EPS)) * g


def _const_spec(shape):
    nd = len(shape)
    return pl.BlockSpec(shape, lambda *_: (0,) * nd, pipeline_mode=pl.Buffered(1))


def _pre_kernel(x_ref, g_ref, w_ref, o_ref, *, col_chunk):
    xn = _rms(x_ref[...], g_ref[...]).astype(BF16)
    n = o_ref.shape[-1]
    for c0 in range(0, n, col_chunk):
        c1 = min(n, c0 + col_chunk)
        o_ref[:, c0:c1] = _mm(xn, w_ref[:, c0:c1])


def _pre(x, g, w, tm):
    t, d = x.shape
    n = w.shape[1]
    return pl.pallas_call(
        functools.partial(_pre_kernel, col_chunk=512),
        grid=(t // tm,),
        in_specs=[pl.BlockSpec((tm, d), lambda i: (i, 0)),
                  _const_spec((1, d)),
                  _const_spec((d, n))],
        out_specs=pl.BlockSpec((tm, n), lambda i: (i, 0)),
        out_shape=jax.ShapeDtypeStruct((t, n), F32),
        compiler_params=pltpu.CompilerParams(dimension_semantics=("parallel",),
                                             vmem_limit_bytes=VMEM_LIMIT),
        name="pre_proj",
    )(x, g.reshape(1, d), w)


def _silu(x):
    return x * jax.nn.sigmoid(x)


_AB_SEGMENTS = ((OFF_DGATE, HW, "silu"), (OFF_GQ, GW, "gla_q_scale"), (OFF_GK, GW, None), (OFF_GV, HW, None),
                (OFF_GGATE, HW, "silu"), (OFF_SMALL, LANE, None))


def _pre_ab_kernel(x_ref, g_ref, w_ref, conv0_ref, cw_ref, o_ref, convn_ref, xp_ref, *, nb, col_chunk):
    i = pl.program_id(0)
    tm = x_ref.shape[0]
    n = SUBLANE + tm
    xn = _rms(x_ref[...], g_ref[...]).astype(BF16)

    @pl.when(i % nb == 0)
    def _stream_start():
        xp_ref[0:SUBLANE, :] = conv0_ref[...]

    cw = cw_ref[...]
    prev = xp_ref[...]
    conv_cols = [slice(c0, c0 + col_chunk) for c0 in range(0, DN_QKV, col_chunk)]
    seg_cols = [(slice(c0, min(c0 + col_chunk, off + width)), epilogue)
                for off, width, epilogue in _AB_SEGMENTS for c0 in range(off, off + width, col_chunk)]

    def conv_chunk(cs, raw_full, half):
        cs = slice(cs.start + half * LANE, cs.start + (half + 1) * LANE)
        raw = raw_full[:, half * LANE:(half + 1) * LANE]
        xa = jnp.concatenate([prev[:, cs], raw], axis=0)
        y = pltpu.roll(xa, n - (SUBLANE - 3), axis=0)[0:tm] * cw[0:1, cs]
        y = y + pltpu.roll(xa, n - (SUBLANE - 2), axis=0)[0:tm] * cw[1:2, cs]
        y = y + pltpu.roll(xa, n - (SUBLANE - 1), axis=0)[0:tm] * cw[2:3, cs]
        y = y + raw * cw[3:4, cs]
        o_ref[:, cs] = _silu(y)
        last = raw[tm - SUBLANE:tm, :]
        xp_ref[:, cs] = last
        convn_ref[:, cs] = last

    def seg_chunk(cs, epilogue):
        r = _mm(xn, w_ref[:, cs])
        if epilogue == "silu":
            r = _silu(r)
        elif epilogue == "gla_q_scale":
            r = r * (GLA_DK ** -0.5)
        o_ref[:, cs] = r

    pending = []
    for j in range(max(len(conv_cols), len(seg_cols))):
        if j < len(conv_cols):
            raw = _mm(xn, w_ref[:, conv_cols[j]])
            if pending:
                conv_chunk(*pending.pop(0))
            pending += [(conv_cols[j], raw, half) for half in range(col_chunk // LANE)]
        if j < len(seg_cols):
            seg_chunk(*seg_cols[j])
            if pending:
                conv_chunk(*pending.pop(0))
    for item in pending:
        conv_chunk(*item)


def _pre_ab(x, g, w, conv0, convw, tm, rows_per_stream):
    t, d = x.shape
    nb = rows_per_stream // tm
    assert nb * tm == rows_per_stream, (rows_per_stream, tm)
    nstreams = t // rows_per_stream
    state_spec = pl.BlockSpec((None, SUBLANE, DN_QKV), lambda i: (i // nb, 0, 0))
    return pl.pallas_call(
        functools.partial(_pre_ab_kernel, nb=nb, col_chunk=256),
        grid=(t // tm,),
        in_specs=[pl.BlockSpec((tm, d), lambda i: (i, 0)),
                  _const_spec((1, d)),
                  _const_spec((d, AB_PAD)),
                  state_spec,
                  _const_spec(convw.shape)],
        out_specs=[pl.BlockSpec((tm, AB_PAD), lambda i: (i, 0)), state_spec],
        out_shape=[jax.ShapeDtypeStruct((t, AB_PAD), F32),
                   jax.ShapeDtypeStruct((nstreams, SUBLANE, DN_QKV), F32)],
        scratch_shapes=[pltpu.VMEM((SUBLANE, DN_QKV), F32)],
        compiler_params=pltpu.CompilerParams(dimension_semantics=("arbitrary",),
                                             vmem_limit_bytes=VMEM_LIMIT),
        name="pre_ab",
    )(x, g.reshape(1, d), w, conv0, convw)


def _pre_qkv_kernel(x_ref, g_ref, w_ref, o_ref, k_ref, v_ref, *, col_chunk):
    xn = _rms(x_ref[...], g_ref[...]).astype(BF16)
    dm = k_ref.shape[-1]
    for c0 in range(0, 3 * dm, col_chunk):
        c1 = c0 + col_chunk
        r = _mm(xn, w_ref[:, c0:c1])
        if c1 <= dm:
            o_ref[:, c0:c1] = (r * ATT_QSCALE).astype(BF16)
        else:
            o_ref[:, c0:c1] = r.astype(BF16)
            kv_ref = k_ref if c1 <= 2 * dm else v_ref
            kv_ref[:, c0 % dm:c0 % dm + col_chunk] = r


def _pre_qkv(x, g, w, tm, rows_per_stream):
    t, d = x.shape
    dm = w.shape[1] // 3
    nb = max(rows_per_stream // tm, 1)
    tail = pl.BlockSpec((tm, dm), lambda i: (i // nb, 0))
    return pl.pallas_call(
        functools.partial(_pre_qkv_kernel, col_chunk=512),
        grid=(t // tm,),
        in_specs=[pl.BlockSpec((tm, d), lambda i: (i, 0)),
                  _const_spec((1, d)),
                  _const_spec((d, 3 * dm))],
        out_specs=[pl.BlockSpec((tm, 3 * dm), lambda i: (i, 0)), tail, tail],
        out_shape=[jax.ShapeDtypeStruct((t, 3 * dm), BF16),
                   jax.ShapeDtypeStruct((t // nb, dm), F32),
                   jax.ShapeDtypeStruct((t // nb, dm), F32)],
        compiler_params=pltpu.CompilerParams(dimension_semantics=("arbitrary",),
                                             vmem_limit_bytes=VMEM_LIMIT),
        name="pre_qkv",
    )(x, g.reshape(1, d), w)


def _post_kernel(o_ref, x_ref, wo_ref, g_ref, wgu_ref, wd_ref, gf_ref, out_ref, acc_ref, *, ff_chunk, final):
    x1 = x_ref[...] + _mm(o_ref[...], wo_ref[...])
    xn = _rms(x1, g_ref[...]).astype(BF16)
    dff = wd_ref.shape[0]
    for f0 in range(0, dff, ff_chunk):
        gate = _mm(xn, wgu_ref[:, f0:f0 + ff_chunk])
        up = _mm(xn, wgu_ref[:, dff + f0:dff + f0 + ff_chunk])
        a = (gate * jax.nn.sigmoid(gate) * up).astype(BF16)
        part = _mm(a, wd_ref[f0:f0 + ff_chunk, :])
        if f0 == 0:
            acc_ref[...] = part
        else:
            acc_ref[...] += part
    x2 = x1 + acc_ref[...]
    if final:
        x2 = _rms(x2, gf_ref[...])
    out_ref[...] = x2


def _post(o, x, wo, g, wgu, wd, gf, tm, final):
    t, d = x.shape
    dm = o.shape[1]
    dff = wd.shape[0]
    return pl.pallas_call(
        functools.partial(_post_kernel, ff_chunk=256, final=final),
        grid=(t // tm,),
        in_specs=[pl.BlockSpec((tm, dm), lambda i: (i, 0)),
                  pl.BlockSpec((tm, d), lambda i: (i, 0)),
                  _const_spec((dm, d)),
                  _const_spec((1, d)),
                  _const_spec((d, 2 * dff)),
                  _const_spec((dff, d)),
                  _const_spec((1, d))],
        out_specs=pl.BlockSpec((tm, d), lambda i: (i, 0)),
        out_shape=jax.ShapeDtypeStruct((t, d), F32),
        scratch_shapes=[pltpu.VMEM((tm, d), F32)],
        compiler_params=pltpu.CompilerParams(dimension_semantics=("parallel",),
                                             vmem_limit_bytes=VMEM_LIMIT),
        name="post_ffn",
    )(o, x, wo, g.reshape(1, d), wgu, wd, gf.reshape(1, d))


def _score_group_of_fg(fg, c):
    return (2 * fg * c) // LANE


@functools.lru_cache(maxsize=None)
def _gla_pairs(c):
    hs = LANE // c
    pairs = []
    for sg in range(H * c // LANE):
        for qd in range(GW // LANE):
            if set(range(sg * hs, (sg + 1) * hs)) & set(range(4 * qd, 4 * qd + 4)):
                pairs.append((sg, qd))
    return tuple(pairs)


@functools.lru_cache(maxsize=None)
def _ab_consts(c):
    w = H * c
    hs = LANE // c
    nsb = c // GLA_SUB
    i = np.arange(c)
    r = np.arange(LANE)
    w_pos = np.arange(w) % c
    ltri = (i[None, :] <= i[:, None])
    eye_w = (i[:, None] == w_pos[None, :])
    causal_w = (w_pos[None, :] <= i[:, None])
    strict_w = (w_pos[None, :] < i[:, None])
    dmask = causal_w & ((i[:, None] // GLA_SUB) == (w_pos[None, :] // GLA_SUB))
    row_hl = r // c
    row_j = r % c
    m_ss = (row_hl[:, None] == row_hl[None, :])
    m_ff = ((r // DN_DV)[:, None] == (r // DN_DV)[None, :])
    m_sf = np.stack([(_score_group_of_fg(fg, c) * hs + row_hl)[:, None] == (2 * fg + r // DN_DV)[None, :]
                     for fg in range(NFG)])
    pairs = _gla_pairs(c)
    m_sq = np.stack([(sg * hs + row_hl)[:, None] == (4 * qd + r // GLA_DK)[None, :] for sg, qd in pairs])
    m_sq3 = np.stack([np.concatenate([m & ((row_j // GLA_SUB) == j)[:, None] for j in range(max(nsb - 1, 1))],
                                     axis=1) for m in m_sq])
    m_fq = np.stack([(2 * fg + r // DN_DV)[:, None] == (4 * (fg // 2) + r // GLA_DK)[None, :]
                     for fg in range(NFG)])
    expg = np.zeros((LANE, HW), np.float32)
    exps = np.zeros((LANE, w), np.float32)
    expb = np.zeros((LANE, HW), np.float32)
    for h in range(H):
        expg[h, DN_DV * h:DN_DV * (h + 1)] = 1.0
        exps[h, c * h:c * (h + 1)] = 1.0
        expb[H + h, DN_DV * h:DN_DV * (h + 1)] = 1.0
    f = lambda a: jnp.asarray(a, F32)
    b = lambda a: jnp.asarray(a, BF16)
    return dict(ltri=b(ltri), eye_w=f(eye_w), causal_w=f(causal_w), strict_w=f(strict_w), dmask=f(dmask),
                m_ss=b(m_ss), m_sf=b(m_sf), m_ffb=b(m_ff), m_ff=f(m_ff), m_sq=b(m_sq), m_sq3=b(m_sq3),
                m_fq=f(m_fq), expg=b(expg), exps=b(exps), expb=b(expb))


_AB_CONST_ORDER = ("ltri", "eye_w", "causal_w", "strict_w", "dmask", "m_ss", "m_sf", "m_ffb", "m_ff",
                   "m_sq", "m_sq3", "m_fq", "expg", "exps", "expb")


def _headsum(xs, bones):
    rows = xs[0].shape[0]
    pieces = [_grp(p, g) for x in xs for p in _split(x, 2) for g in range(NFG)]
    half = len(pieces) // 2
    y = jnp.concatenate(_group_mm([jnp.concatenate(pieces[:half], axis=0), jnp.concatenate(pieces[half:], axis=0)],
                                 [bones, bones]), axis=0)
    outs = []
    for n in range(len(xs)):
        base = 2 * NFG * n
        outs.append(jnp.concatenate(
            [y[(base + g) * rows:(base + g + 1) * rows] + y[(base + NFG + g) * rows:(base + NFG + g + 1) * rows]
             for g in range(NFG)], axis=1))
    return outs


def _ab_kernel(proj_ref, dn0_ref, gl0_ref, gparam_ref, w2_ref, gkb_ref, onorm_ref,
               ltri_ref, eye_ref, causal_ref, strict_ref, dmask_ref, mss_ref, msf_ref, mffb_ref, mff_ref,
               msq_ref, msq3_ref, mfq_ref, expg_ref, exps_ref, expb_ref,
               o_ref, dns_ref, gls_ref,
               sdn_ref, sgl_ref, *, c, nc, ns):
    i = pl.program_id(1)
    nsb = c // GLA_SUB
    hs = LANE // c
    nsg = H * c // LANE
    sg_of = [_score_group_of_fg(fg, c) for fg in range(NFG)]
    streams = range(ns)

    @pl.when(i == 0)
    def _load_state():
        sdn_ref[...] = dn0_ref[...]
        sgl_ref[...] = gl0_ref[...]

    bones = mffb_ref[...]
    mss = mss_ref[...]
    ltri = ltri_ref[...]
    gp = gparam_ref[...]
    onorm = onorm_ref[...]

    q = [proj_ref[s, :, OFF_QKV:OFF_QKV + HW] for s in streams]
    k = [proj_ref[s, :, OFF_QKV + HW:OFF_QKV + 2 * HW] for s in streams]
    v = [proj_ref[s, :, OFF_QKV + 2 * HW:OFF_QKV + 3 * HW] for s in streams]

    ssq = _headsum([q[s] * q[s] for s in streams] + [k[s] * k[s] for s in streams], bones)
    for s in streams:
        q[s] = q[s] * lax.rsqrt(ssq[s] + EPS) * (DN_DK ** -0.5)
        k[s] = k[s] * lax.rsqrt(ssq[ns + s] + EPS)

    pairs = _gla_pairs(c)
    small = [proj_ref[s, :, OFF_SMALL:OFF_SMALL + LANE] for s in streams]
    gl = {}

    def gla_gate_logits():
        z = _mm(jnp.concatenate([small[s].astype(BF16) for s in streams], axis=0), w2_ref[...]) + gkb_ref[...]
        gl["z"] = [z[c * s:c * (s + 1)] for s in streams]

    def gla_cumsum():
        glog = jnp.concatenate([-_softplus(-gl["z"][s]) * (1.0 / GLA_NORMALIZER) for s in streams], axis=1)
        bcum = _mmx_left(ltri, glog, 3)
        gl["bcum"] = [bcum[:, GW * s:GW * (s + 1)] for s in streams]

    def gla_scores_and_state():
        gl["a"], gl["o_inter"], gl["v2b"] = [], [], []
        for s in streams:
            bcum = gl["bcum"][s]
            q2 = proj_ref[s, :, OFF_GQ:OFF_GQ + GW]
            k2 = proj_ref[s, :, OFF_GK:OFF_GK + GW]
            v2b = proj_ref[s, :, OFF_GV:OFF_GV + HW].astype(BF16)
            blast = bcum[c - 1:c, :]
            qe = (q2 * jnp.exp(bcum)).astype(BF16)
            kdec = (k2 * jnp.exp(blast - bcum)).astype(BF16)
            eblast = jnp.exp(blast)
            rmid = jnp.concatenate(
                [jnp.broadcast_to(bcum[GLA_SUB * t + GLA_SUB // 2:GLA_SUB * t + GLA_SUB // 2 + 1, :], (GLA_SUB, GW))
                 for t in range(nsb)], axis=0)
            qm = (q2 * jnp.exp(bcum - rmid)).astype(BF16)
            km = (k2 * jnp.exp(rmid - bcum)).astype(BF16)
            parts = [None] * nsg
            prods = _group_mm([_grp(qm, qd) for _, qd in pairs],
                             [_tile_rows(_grp(km, qd), hs) * msq_ref[n] for n, (_, qd) in enumerate(pairs)], nt=True)
            for n, (sg, _) in enumerate(pairs):
                parts[sg] = prods[n] if parts[sg] is None else parts[sg] + prods[n]
            a_gl = dmask_ref[...] * jnp.concatenate(parts, axis=1)
            if nsb > 1:
                rend = jnp.concatenate(
                    [jnp.broadcast_to(bcum[GLA_SUB * (t + 1) - 1:GLA_SUB * (t + 1), :], (GLA_SUB, GW))
                     for t in range(nsb)], axis=0)
                kr = (k2 * jnp.exp(rend - bcum)).astype(BF16)
                rowi = lax.broadcasted_iota(jnp.int32, (c, GW), 0)
                qs = []
                for t in range(nsb - 1):
                    e = jnp.where(rowi >= GLA_SUB * (t + 1), bcum - rend[GLA_SUB * t:GLA_SUB * t + 1, :], -jnp.inf)
                    qs.append((q2 * jnp.exp(e)).astype(BF16))
                parts = [None] * nsg
                prods = _group_mm(
                    [jnp.concatenate([_grp(x, qd) for x in qs], axis=1) for _, qd in pairs],
                    [_tile_rows(jnp.concatenate([_grp(kr, qd)] * (nsb - 1), axis=1), hs) * msq3_ref[n]
                     for n, (_, qd) in enumerate(pairs)], nt=True)
                for n, (sg, _) in enumerate(pairs):
                    parts[sg] = prods[n] if parts[sg] is None else parts[sg] + prods[n]
                a_gl = a_gl + jnp.concatenate(parts, axis=1)
            st = [sgl_ref[s, fg] for fg in range(NFG)]
            o_inter = _group_mm([_grp(qe, fg // 2) for fg in range(NFG)], [x.astype(BF16) for x in st], nt=True)
            for fg in range(NFG):
                qd = fg // 2
                sgl_ref[s, fg] = st[fg] * _grp(eblast, qd) + _mm_tn(_grp(v2b, fg), _grp(kdec, qd)) * mfq_ref[fg]
            gl["a"].append(a_gl.astype(BF16))
            gl["o_inter"].append(o_inter)
            gl["v2b"].append(v2b)

    def gla_out():
        gl["o"] = []
        for s in streams:
            intra = _group_mm([_grp(gl["a"][s], sg_of[fg]) for fg in range(NFG)],
                             [_tile_rows(_grp(gl["v2b"][s], fg), hs) * msf_ref[fg] for fg in range(NFG)])
            gl["o"].append(jnp.concatenate([gl["o_inter"][s][fg] + intra[fg] for fg in range(NFG)], axis=1))

    def rows_of(x, s):
        return x[c * s:c * (s + 1)]

    gla_gate_logits()
    g_all = jnp.concatenate([-jnp.exp(gp[0:1, :]) * _softplus(small[s] + gp[1:2, :]) for s in streams], axis=1)
    gcum_all = _mmx_left(ltri, g_all, 3)
    gla_cumsum()
    gcum_parts = _split(jnp.concatenate([_grp(gcum_all, s) for s in streams], axis=0), 3)
    gw_all = _mm(gcum_parts[0], expg_ref[...]) + _mm(gcum_parts[1], expg_ref[...]) + _mm(gcum_parts[2], expg_ref[...])
    if c == DN_DV:
        gs_all = gw_all
    else:
        gs_all = (_mm(gcum_parts[0], exps_ref[...]) + _mm(gcum_parts[1], exps_ref[...])
                  + _mm(gcum_parts[2], exps_ref[...]))
    bw_all = _mmx(jnp.concatenate([jax.nn.sigmoid(small[s]) for s in streams], axis=0), expb_ref[...], 2)
    gcum_w = [rows_of(gw_all, s) for s in streams]
    beta_w = [rows_of(bw_all, s) for s in streams]
    eg_w = [jnp.exp(gcum_w[s]) for s in streams]
    edec_w = [jnp.exp(gcum_w[s][c - 1:c, :] - gcum_w[s]) for s in streams]
    eglast_w = [jnp.exp(gcum_w[s][c - 1:c, :]) for s in streams]
    gla_scores_and_state()

    decay_w = []
    for s in streams:
        gcum_s = rows_of(gs_all, s)
        d_w = gcum_s - jnp.sum(gcum_s * eye_ref[...], axis=0, keepdims=True)
        decay_w.append(jnp.exp(jnp.where(causal_ref[...] > 0.0, d_w, -jnp.inf)))

    kb = [k[s] * beta_w[s] for s in streams]
    a_w, qkd_b = [], []
    for s in streams:
        kq = jnp.concatenate([kb[s], q[s]], axis=0).astype(BF16)
        kbf = k[s].astype(BF16)
        parts = [None] * nsg
        prods = _group_mm([_grp(kq, fg) for fg in range(NFG)],
                         [_tile_rows(_grp(kbf, fg), hs) * msf_ref[fg] for fg in range(NFG)], nt=True)
        for fg in range(NFG):
            parts[sg_of[fg]] = prods[fg] if parts[sg_of[fg]] is None else parts[sg_of[fg]] + prods[fg]
        r = jnp.concatenate(parts, axis=1)
        a_w.append(strict_ref[...] * r[0:c] * decay_w[s])
        qkd_b.append((r[c:2 * c] * decay_w[s]).astype(BF16))

    def ssprod(x_w, y_w):
        xb = x_w.astype(BF16)
        yb = y_w.astype(BF16)
        return jnp.concatenate(_group_mm([_grp(xb, sg) for sg in range(nsg)],
                                        [_tile_rows(_grp(yb, sg), hs) * mss for sg in range(nsg)]), axis=1)

    nlev = int(math.log2(c))
    pw = [-a_w[s] for s in streams]
    t_w = [eye_ref[...] + pw[s] for s in streams]
    pw = [ssprod(pw[s], pw[s]) for s in streams]
    gla_out()
    for _ in range(2, nlev):
        rr = [ssprod(jnp.concatenate([pw[s], t_w[s]], axis=0), pw[s]) for s in streams]
        t_w = [t_w[s] + rr[s][c:2 * c] for s in streams]
        pw = [rr[s][0:c] for s in streams]
    rr = [ssprod(t_w[s], pw[s]) for s in streams]
    tb = [(t_w[s] + rr[s]).astype(BF16) for s in streams]

    vbb = [(v[s] * beta_w[s]).astype(BF16) for s in streams]
    kbe = [(kb[s] * eg_w[s]).astype(BF16) for s in streams]
    qeg = [q[s] * eg_w[s] for s in streams]
    kdc = [(k[s] * edec_w[s]).astype(BF16) for s in streams]
    fgs = range(NFG)
    uw, rs, s_dn, vnb = {}, {}, {}, {}
    for s in streams:
        uw[s] = _group_mm([_grp(tb[s], sg_of[fg]) for fg in fgs],
                         [jnp.concatenate([_tile_rows(_grp(vbb[s], fg), hs) * msf_ref[fg],
                                           _tile_rows(_grp(kbe[s], fg), hs) * msf_ref[fg]], axis=1) for fg in fgs])
    for s in streams:
        s_dn[s] = [sdn_ref[s, fg] for fg in fgs]
        rs[s] = _group_mm([jnp.concatenate([uw[s][fg][:, LANE:2 * LANE], _grp(qeg[s], fg)], axis=0).astype(BF16)
                          for fg in fgs], [s_dn[s][fg].astype(BF16) for fg in fgs])
    o_dn = []
    for s in streams:
        vnb[s] = [(uw[s][fg][:, 0:LANE] - rs[s][fg][0:c]).astype(BF16) for fg in fgs]
        intra = _group_mm([_grp(qkd_b[s], sg_of[fg]) for fg in fgs],
                         [_tile_rows(vnb[s][fg], hs) * msf_ref[fg] for fg in fgs])
        o_dn.append(jnp.concatenate([rs[s][fg][c:2 * c] + intra[fg] for fg in fgs], axis=1))
    for s in streams:
        for fg in fgs:
            sdn_ref[s, fg] = (s_dn[s][fg] * _grp(eglast_w[s], fg)
                              + _mm_tn(_grp(kdc[s], fg), vnb[s][fg]) * mff_ref[...])
    o_gl = gl["o"]

    oss = _headsum([o_dn[s] * o_dn[s] for s in streams] + [o_gl[s] * o_gl[s] for s in streams], bones)
    for s in streams:
        md, mgl = oss[s], oss[ns + s]
        gd = proj_ref[s, :, OFF_DGATE:OFF_DGATE + HW]
        gg = proj_ref[s, :, OFF_GGATE:OFF_GGATE + HW]
        y_dn = (o_dn[s] * lax.rsqrt(md * (1.0 / DN_DV) + EPS)) * onorm[0:1, :] * gd
        y_gl = (o_gl[s] * lax.rsqrt(mgl * (1.0 / GLA_DV) + EPS)) * onorm[1:2, :] * gg
        o_ref[s, :, 0:HW] = y_dn.astype(BF16)
        o_ref[s, :, HW:2 * HW] = y_gl.astype(BF16)

    @pl.when(i == nc - 1)
    def _store_state():
        dns_ref[...] = sdn_ref[...]
        gls_ref[...] = sgl_ref[...]


def _ab_mixer(proj, dn0, gl0, gparam, w2p, gkb, onorm, c):
    b, l, _ = proj.shape
    nc = l // c
    ns = AB_STREAMS
    consts = _ab_consts(c)
    cvals = [consts[n] for n in _AB_CONST_ORDER]
    per_b = lambda shape: pl.BlockSpec((ns,) + shape, lambda bi, ci: (bi,) + (0,) * len(shape))
    state = (NFG, LANE, LANE)
    in_specs = ([pl.BlockSpec((ns, c, AB_PAD), lambda bi, ci: (bi, ci, 0)),
                 per_b(state), per_b(state),
                 _const_spec(gparam.shape), _const_spec(w2p.shape),
                 _const_spec(gkb.shape), _const_spec(onorm.shape)]
                + [_const_spec(a.shape) for a in cvals])
    out_specs = [pl.BlockSpec((ns, c, 2 * HW), lambda bi, ci: (bi, ci, 0)),
                 per_b(state), per_b(state)]
    out_shape = [jax.ShapeDtypeStruct((b, l, 2 * HW), BF16),
                 jax.ShapeDtypeStruct((b,) + state, F32),
                 jax.ShapeDtypeStruct((b,) + state, F32)]
    return pl.pallas_call(
        functools.partial(_ab_kernel, c=c, nc=nc, ns=ns),
        grid=(b // ns, nc),
        in_specs=in_specs,
        out_specs=out_specs,
        out_shape=out_shape,
        scratch_shapes=[pltpu.VMEM((ns,) + state, F32),
                        pltpu.VMEM((ns,) + state, F32)],
        compiler_params=pltpu.CompilerParams(dimension_semantics=("parallel", "arbitrary"),
                                             vmem_limit_bytes=VMEM_LIMIT),
        name="ab_mixer",
    )(proj, dn0, gl0, gparam, w2p, gkb, onorm, *cvals)


_DN_PLACE = np.eye(2, dtype=np.float32)
_GLA_PLACE = np.zeros((2, 2, 4), np.float32)
for _fp in range(2):
    for _hl in range(2):
        _GLA_PLACE[_fp, _hl, 2 * _fp + _hl] = 1.0


def _dn_state_to_groups(s):
    b = s.shape[0]
    s5 = s.reshape(b, NFG, 2, DN_DK, 1, DN_DV)
    return (s5 * _DN_PLACE[None, None, :, None, :, None]).reshape(b, NFG, LANE, LANE)


def _dn_state_from_groups(g):
    b = g.shape[0]
    g6 = g.reshape(b, NFG, 2, DN_DK, 2, DN_DV)
    return jnp.sum(g6 * _DN_PLACE[None, None, :, None, :, None], axis=4).reshape(b, DN_HEADS, DN_DK, DN_DV)


def _gla_state_to_groups(s):
    b = s.shape[0]
    st = jnp.swapaxes(s, -1, -2).reshape(b, 2, 2, 2, GLA_DV, 1, GLA_DK)
    return (st * _GLA_PLACE[None, None, :, :, None, :, None]).reshape(b, NFG, LANE, LANE)


def _gla_state_from_groups(g):
    b = g.shape[0]
    g7 = g.reshape(b, 2, 2, 2, GLA_DV, 4, GLA_DK)
    st = jnp.sum(g7 * _GLA_PLACE[None, None, :, :, None, :, None], axis=5)
    return jnp.swapaxes(st.reshape(b, GLA_HEADS, GLA_DV, GLA_DK), -1, -2)


def _softmax2_rows(parts):
    m = None
    for s in parts:
        t = jnp.max(s, axis=-1, keepdims=True)
        m = t if m is None else jnp.maximum(m, t)
    es = [jnp.exp2(s - m) for s in parts]
    l = None
    for e in es:
        t = jnp.sum(e, axis=-1, keepdims=True)
        l = t if l is None else l + t
    inv = 1.0 / l
    return [e * inv for e in es]


def _fold_lanes(x, op):
    out = x[:, 0:LANE]
    for c0 in range(LANE, x.shape[1], LANE):
        out = op(out, x[:, c0:c0 + LANE])
    return out


def _attn_prompt_kernel(q_ref, k0_ref, k1_ref, k2_ref, v0_ref, v1_ref, v2_ref, bias_ref, o_ref, *, qb):
    i = pl.program_id(1)
    offs = (jnp.where(i >= 2, 0.0, -jnp.inf), jnp.where(i >= 1, 0.0, -jnp.inf), 0.0)
    k_refs = (k0_ref, k1_ref, k2_ref)
    v_refs = (v0_ref, v1_ref, v2_ref)

    nq = qb // CHUNK
    tpb = qb // LANE
    tile_of, _, _, row_const = _bias_tiles(qb)

    def bias_tile(h, qc, lt):
        u = tile_of[qc, lt]
        return bias_ref[h, u, 0:1, :] if row_const[u] else bias_ref[h, u]

    def scores(h):
        sl = slice(C_HD * h, C_HD * (h + 1))
        qh = q_ref[:, sl]
        return [_mm_nt(qh, k_refs[t][:, sl]) for t in range(3)]

    def finish(h, ss):
        sl = slice(C_HD * h, C_HD * (h + 1))
        s_t, ems = {}, []
        for qc in range(nq):
            rows = slice(CHUNK * qc, CHUNK * (qc + 1))
            em = None
            for t in range(3):
                et = None
                for j in range(tpb):
                    lt = t * tpb + j
                    if (qc, lt) in tile_of:
                        s = ss[t][rows, LANE * j:LANE * (j + 1)] + bias_tile(h, qc, lt)
                        s_t[qc, lt] = s
                        et = s if et is None else jnp.maximum(et, s)
                if et is not None:
                    et = et + offs[t]
                    em = et if em is None else jnp.maximum(em, et)
            ems.append(em)
        m = jnp.max(jnp.concatenate(ems, axis=0), axis=-1, keepdims=True)
        shift = [m - offs[t] for t in range(3)]
        p_tiles, els = [], []
        for qc in range(nq):
            rows = slice(CHUNK * qc, CHUNK * (qc + 1))
            el, row_tiles = None, []
            for lt in range(3 * tpb):
                if (qc, lt) in s_t:
                    p = jnp.exp2(s_t[qc, lt] - shift[lt // tpb][rows])
                    el = p if el is None else el + p
                    row_tiles.append(p.astype(BF16))
                else:
                    row_tiles.append(jnp.zeros((CHUNK, LANE), BF16))
            els.append(el)
            p_tiles.append(row_tiles)
        inv_l = [1.0 / jnp.sum(jnp.concatenate(els, axis=0), axis=-1, keepdims=True)]
        acc = None
        for t in range(3):
            p_blk = jnp.concatenate(
                [jnp.concatenate(p_tiles[qc][t * tpb:(t + 1) * tpb], axis=1) for qc in range(nq)], axis=0)
            at = _mm(p_blk, v_refs[t][:, sl])
            acc = at if acc is None else acc + at
        o_ref[:, sl] = (acc * jnp.concatenate(inv_l, axis=0)).astype(BF16)

    ss_next = scores(0)
    for h in range(C_HEADS):
        ss = ss_next
        if h + 1 < C_HEADS:
            ss_next = scores(h + 1)
        finish(h, ss)


def _attn_prompt(qkv, bias, qb):
    b, l, _ = qkv.shape
    dm = C_HEADS * C_HD
    kv_spec = lambda colblk, back: pl.BlockSpec(
        (None, qb, dm), lambda bi, i: (bi, jnp.maximum(i - back, 0), colblk))
    return pl.pallas_call(
        functools.partial(_attn_prompt_kernel, qb=qb),
        grid=(b, l // qb),
        in_specs=[pl.BlockSpec((None, qb, dm), lambda bi, i: (bi, i, 0)),
                  kv_spec(1, 2), kv_spec(1, 1), kv_spec(1, 0),
                  kv_spec(2, 2), kv_spec(2, 1), kv_spec(2, 0),
                  _const_spec(bias.shape)],
        out_specs=pl.BlockSpec((None, qb, dm), lambda bi, i: (bi, i, 0)),
        out_shape=jax.ShapeDtypeStruct((b, l, dm), BF16),
        compiler_params=pltpu.CompilerParams(dimension_semantics=("parallel", "parallel"),
                                             vmem_limit_bytes=VMEM_LIMIT),
        name="attn_prompt",
    )(qkv, qkv, qkv, qkv, qkv, qkv, qkv, bias)


def _attn_sample_kernel(q_ref, k_ref, v_ref, ck_ref, cv_ref, bc_ref, bn_ref, o_ref):
    sls = [slice(C_HD * h, C_HD * (h + 1)) for h in range(C_HEADS)]
    scores = [(_mm_nt(q_ref[:, sl], ck_ref[:, sl].astype(BF16)) + bc_ref[h],
               _mm_nt(q_ref[:, sl], k_ref[:, sl]) + bn_ref[h]) for h, sl in enumerate(sls)]
    probs = [_softmax2_rows(list(s)) for s in scores]
    for (pc, pn), sl in zip(probs, sls):
        oh = _mm(pc.astype(BF16), cv_ref[:, sl].astype(BF16)) + _mm(pn.astype(BF16), v_ref[:, sl])
        o_ref[:, sl] = oh.astype(BF16)


def _attn_sample(qkv, ck, cv, bias_c, bias_n):
    b, l, _ = qkv.shape
    dm = C_HEADS * C_HD
    ncache = ck.shape[1]
    new_spec = lambda colblk: pl.BlockSpec((None, l, dm), lambda bi: (bi, 0, colblk))
    cache_spec = pl.BlockSpec((None, ncache, dm), lambda bi: (bi, 0, 0))
    return pl.pallas_call(
        _attn_sample_kernel,
        grid=(b,),
        in_specs=[new_spec(0), new_spec(1), new_spec(2), cache_spec, cache_spec,
                  _const_spec(bias_c.shape), _const_spec(bias_n.shape)],
        out_specs=pl.BlockSpec((None, l, dm), lambda bi: (bi, 0, 0)),
        out_shape=jax.ShapeDtypeStruct((b, l, dm), BF16),
        compiler_params=pltpu.CompilerParams(dimension_semantics=("parallel",),
                                             vmem_limit_bytes=VMEM_LIMIT),
        name="attn_sample",
    )(qkv, qkv, qkv, ck, cv, bias_c, bias_n)


@functools.lru_cache(maxsize=None)
def _bias_tiles(qb):
    nq, ntile, cpt = qb // CHUNK, 3 * qb // LANE, LANE // CHUNK
    r = np.arange(CHUNK)[:, None]
    col = np.arange(LANE)[None, :]
    n = np.arange(CHUNK + LANE - 1)
    ids, tile_of, gens, masks, row_const = {}, {}, [], [], []
    for qc in range(nq):
        lo, hi = qc + 2 * nq - C_BAND_CHUNKS, qc + 2 * nq
        for lt in range(ntile):
            if cpt * lt + cpt - 1 < lo or cpt * lt > hi:
                continue
            base = CHUNK * qc + 2 * qb - LANE * lt
            rel = np.clip(base + r - col, -MAX_REL, MAX_REL) + MAX_REL
            kc = (LANE * lt + col) // CHUNK
            band = np.broadcast_to((kc >= lo) & (kc <= hi), (CHUNK, LANE))
            key = (rel.tobytes(), band.tobytes())
            if key not in ids:
                ids[key] = len(gens)
                gens.append(np.clip(base + CHUNK - 1 - n, -MAX_REL, MAX_REL) + MAX_REL)
                masks.append(band)
                row_const.append(bool((rel == rel[0:1]).all()))
            tile_of[qc, lt] = ids[key]
    return tile_of, np.stack(gens), np.stack(masks), tuple(row_const)


def _prompt_bias(table, qb):
    _, gens, masks, _ = _bias_tiles(qb)
    nh, nu = table.shape[0], gens.shape[0]
    period = CHUNK + LANE
    w = jnp.concatenate([table[:, gens], jnp.zeros((nh, nu, 1), table.dtype)], axis=2)
    w = jnp.roll(w, -(CHUNK - 1), axis=2)
    tiles = jnp.tile(w, (1, 1, CHUNK))[:, :, :CHUNK * (period - 1)].reshape(nh, nu, CHUNK, period - 1)[..., :LANE]
    return jnp.where(jnp.asarray(masks)[None], tiles.astype(F32) * LOG2E, -jnp.inf)


def _sample_bias(table, l, ncache):
    qpos = PAST_LEN + np.arange(l)
    kpos = np.concatenate([PAST_LEN - ncache + np.arange(ncache), PAST_LEN + np.arange(l)])
    rel = np.clip(qpos[:, None] - kpos[None, :], -MAX_REL, MAX_REL) + MAX_REL
    qch = qpos // CHUNK
    kch = kpos // CHUNK
    valid = (kch[None, :] <= qch[:, None]) & (kch[None, :] >= qch[:, None] - C_BAND_CHUNKS)
    bias = jnp.where(jnp.asarray(valid)[None], table[:, rel].astype(F32) * LOG2E, -jnp.inf)
    return bias[:, :, :ncache], bias[:, :, ncache:]


def _permute_w_in(w):
    d = w.shape[0]
    o_a = DN_QKV
    o_b = o_a + DN_HEADS
    o_gate = o_b + DN_HEADS
    o_gq = o_gate + HW
    o_gk = o_gq + GW
    o_gv = o_gk + GW
    o_lr = o_gv + HW
    o_gg = o_lr + GLA_RANK
    return jnp.concatenate(
        [w[:, :DN_QKV], w[:, o_gate:o_gq], w[:, o_gq:o_gk], w[:, o_gk:o_gv], w[:, o_gv:o_lr],
         w[:, o_gg:o_gg + HW], w[:, o_a:o_gate], w[:, o_lr:o_gg],
         jnp.zeros((d, LANE - 2 * DN_HEADS - GLA_RANK), w.dtype)], axis=1)


def _trunk(x, conv0, dn0, gla0, ck, cv, w, prompt, tm):
    b, l, d = x.shape
    depth = w["ffn_norm"].shape[0]
    c = min(CHUNK, l)
    xf = x.reshape(b * l, d)
    convs, dns, glas, ks, vs = [], [], [], [], []
    for layer in range(depth):
        i = layer // 2
        if layer % 2 == 0:
            if conv0 is None:
                conv0p = jnp.zeros((b, SUBLANE, DN_QKV), F32)
                dn0g = gl0g = jnp.zeros((b, NFG, LANE, LANE), F32)
            else:
                conv0p = jnp.pad(conv0[i], ((0, 0), (SUBLANE - (CONV_W - 1), 0), (0, 0)))
                dn0g, gl0g = _dn_state_to_groups(dn0[i]), _gla_state_to_groups(gla0[i])
            proj, convn = _pre_ab(xf, w["ab_norm"][i], w["ab_w_in"][i], conv0p, w["dn_conv_w"][i], min(tm, l), l)
            o, dn_s, gl_s = _ab_mixer(proj.reshape(b, l, AB_PAD), dn0g, gl0g, w["gparam"][i],
                                      w["gk_w2p"][i], w["gla_gk_b"][i], w["onorm"][i], c)
            convs.append(convn[:, SUBLANE - (CONV_W - 1):, :])
            dns.append(_dn_state_from_groups(dn_s))
            glas.append(_gla_state_from_groups(gl_s))
            wo = w["ab_w_out"][i]
        else:
            dm = C_HEADS * C_HD
            rows = min(C_PAST, l) if prompt else l
            assert l <= tm or rows == tm, (l, tm, rows)
            qkv, k_rows, v_rows = _pre_qkv(xf, w["c_norm"][i], w["c_w_qkv"][i], tm, l)
            qkv = qkv.reshape(b, l, 3 * dm)
            ks.append(k_rows.reshape(b, -1, C_HEADS, C_HD)[:, -rows:])
            vs.append(v_rows.reshape(b, -1, C_HEADS, C_HD)[:, -rows:])
            if prompt:
                o = _attn_prompt(qkv, _prompt_bias(w["c_rel_bias"][i], ATT_QB), ATT_QB)
            else:
                ncache = ck.shape[2]
                bias_c, bias_n = _sample_bias(w["c_rel_bias"][i], l, ncache)
                o = _attn_sample(qkv, ck[i].reshape(b, ncache, dm), cv[i].reshape(b, ncache, dm), bias_c, bias_n)
            wo = w["c_w_out"][i]
        xf = _post(o.reshape(b * l, -1), xf, wo, w["ffn_norm"][layer], w["ffn_w_gu"][layer],
                   w["ffn_w_down"][layer], w["final_norm"], tm, final=(layer == depth - 1))
    return (xf.reshape(b, l, d), jnp.stack(convs), jnp.stack(dns), jnp.stack(glas), jnp.stack(ks), jnp.stack(vs))


def kernel(x_prompt, x_sample, state_dn_conv, state_dn, state_gla, cache_c_k, cache_c_v, ab_norm, ab_w_in, dn_conv_w, dn_a_log, dn_dt_bias, dn_out_norm, gla_gk_w2, gla_gk_b, gla_out_norm, ab_w_out, c_norm, c_w_qkv, c_rel_bias, c_w_out, ffn_norm, ffn_w_gu, ffn_w_down, final_norm):
    n_ab = ab_w_in.shape[0]
    gparam = jnp.zeros((n_ab, SUBLANE, LANE), F32)
    gparam = gparam.at[:, 0, :DN_HEADS].set(dn_a_log).at[:, 1, :DN_HEADS].set(dn_dt_bias)
    gk_w2p = jnp.zeros((n_ab, LANE, GW), F32).at[:, 2 * DN_HEADS:2 * DN_HEADS + GLA_RANK, :].set(gla_gk_w2)
    onorm = jnp.stack([jnp.tile(dn_out_norm, (1, DN_HEADS)), jnp.tile(gla_out_norm, (1, GLA_HEADS))], axis=1)
    w = {
        "ab_norm": ab_norm, "c_norm": c_norm, "ffn_norm": ffn_norm, "final_norm": final_norm,
        "ab_w_in": jnp.stack([_permute_w_in(ab_w_in[i]) for i in range(n_ab)]).astype(BF16),
        "dn_conv_w": dn_conv_w, "gparam": gparam, "gk_w2p": gk_w2p.astype(BF16),
        "gla_gk_b": gla_gk_b[:, None, :], "onorm": onorm,
        "ab_w_out": ab_w_out.astype(BF16), "c_w_qkv": c_w_qkv.astype(BF16), "c_rel_bias": c_rel_bias,
        "c_w_out": c_w_out.astype(BF16), "ffn_w_gu": ffn_w_gu.astype(BF16), "ffn_w_down": ffn_w_down.astype(BF16),
    }
    y_p, conv_p, dn_p, gla_p, ck_p, cv_p = _trunk(x_prompt, None, None, None, None, None, w, True, 512)
    ts = x_sample.shape[0] * x_sample.shape[1]
    y_s, conv_s, dn_s, gla_s, ck_s, cv_s = _trunk(x_sample, state_dn_conv, state_dn, state_gla,
                                                  cache_c_k, cache_c_v, w, False, ts)
    return (y_p, y_s, conv_p, conv_s, dn_p, dn_s, gla_p, gla_s, ck_p, ck_s, cv_p, cv_s)
```

```python
import functools
import math

import numpy as np
import jax
import jax.numpy as jnp
from jax import lax
from jax.experimental import pallas as pl
from jax.experimental.pallas import tpu as pltpu

F32 = jnp.float32
BF16 = jnp.bfloat16

EPS = 1e-6
CHUNK = 64
PAST_LEN = 2048
D_MODEL = 1024
DN_HEADS = 8
DN_DK = 64
DN_DV = 64
CONV_W = 4
DN_QKV = DN_HEADS * (2 * DN_DK + DN_DV)
GLA_HEADS = 8
GLA_DK = 32
GLA_DV = 64
GLA_RANK = 16
GLA_NORMALIZER = 16.0
GLA_SUB = 16
C_HEADS = 16
C_HD = 64
C_BAND_CHUNKS = 8
C_PAST = C_BAND_CHUNKS * CHUNK
MAX_REL = 128
ATT_QB = 256
LOG2E = math.log2(math.e)
ATT_QSCALE = C_HD ** -0.5 * LOG2E

H = DN_HEADS
HW = DN_HEADS * DN_DV
GW = GLA_HEADS * GLA_DK
LANE = 128
SUBLANE = 8
NFG = HW // LANE
AB_STREAMS = 4

OFF_QKV = 0
OFF_DGATE = DN_QKV
OFF_GQ = OFF_DGATE + HW
OFF_GK = OFF_GQ + GW
OFF_GV = OFF_GK + GW
OFF_GGATE = OFF_GV + HW
OFF_SMALL = OFF_GGATE + HW
AB_PAD = OFF_SMALL + LANE

VMEM_LIMIT = 56 * 1024 * 1024


def _mm(a, b):
    return jnp.dot(a, b, preferred_element_type=F32)


def _mm_nt(a, b):
    return lax.dot_general(a, b, (((1,), (1,)), ((), ())), preferred_element_type=F32)


def _mm_tn(a, b):
    return lax.dot_general(a, b, (((0,), (0,)), ((), ())), preferred_element_type=F32)


def _split(x, n):
    parts = []
    r = x
    for t in range(n):
        p = r.astype(BF16)
        parts.append(p)
        if t + 1 < n:
            r = r - p.astype(F32)
    return parts


def _mmx(x, m, n):
    out = None
    for p in _split(x, n):
        t = _mm(p, m)
        out = t if out is None else out + t
    return out


def _mmx_left(m, x, n):
    out = None
    for p in _split(x, n):
        t = _mm(m, p)
        out = t if out is None else out + t
    return out


def _softplus(x):
    return jnp.maximum(x, 0.0) + jnp.log(1.0 + jnp.exp(-jnp.abs(x)))


def _tile_rows(x, n):
    return jnp.concatenate([x] * n, axis=0)


def _grp(x, g):
    return x[:, LANE * g:LANE * (g + 1)]


def _group_mm(lhs, rhs, nt=False):
    return [_mm_nt(a, b) if nt else _mm(a, b) for a, b in zip(lhs, rhs)]


def _rms(x, g):
    ms = jnp.mean(x * x, axis=-1, keepdims=True)
    return (x * lax.rsqrt(ms + EPS)) * g


def _const_spec(shape):
    nd = len(shape)
    return pl.BlockSpec(shape, lambda *_: (0,) * nd, pipeline_mode=pl.Buffered(1))


def _pre_kernel(x_ref, g_ref, w_ref, o_ref, *, col_chunk):
    xn = _rms(x_ref[...], g_ref[...]).astype(BF16)
    n = o_ref.shape[-1]
    for c0 in range(0, n, col_chunk):
        c1 = min(n, c0 + col_chunk)
        o_ref[:, c0:c1] = _mm(xn, w_ref[:, c0:c1])


def _pre(x, g, w, tm):
    t, d = x.shape
    n = w.shape[1]
    return pl.pallas_call(
        functools.partial(_pre_kernel, col_chunk=512),
        grid=(t // tm,),
        in_specs=[pl.BlockSpec((tm, d), lambda i: (i, 0)),
                  _const_spec((1, d)),
                  _const_spec((d, n))],
        out_specs=pl.BlockSpec((tm, n), lambda i: (i, 0)),
        out_shape=jax.ShapeDtypeStruct((t, n), F32),
        compiler_params=pltpu.CompilerParams(dimension_semantics=("parallel",),
                                             vmem_limit_bytes=VMEM_LIMIT),
        name="pre_proj",
    )(x, g.reshape(1, d), w)


def _silu(x):
    return x * jax.nn.sigmoid(x)


_AB_SEGMENTS = ((OFF_DGATE, HW, "silu"), (OFF_GQ, GW, "gla_q_scale"), (OFF_GK, GW, None), (OFF_GV, HW, None),
                (OFF_GGATE, HW, "silu"), (OFF_SMALL, LANE, None))


def _pre_ab_kernel(x_ref, g_ref, w_ref, conv0_ref, cw_ref, o_ref, convn_ref, xp_ref, *, nb, col_chunk):
    i = pl.program_id(0)
    tm = x_ref.shape[0]
    n = SUBLANE + tm
    xn = _rms(x_ref[...], g_ref[...]).astype(BF16)

    @pl.when(i % nb == 0)
    def _stream_start():
        xp_ref[0:SUBLANE, :] = conv0_ref[...]

    cw = cw_ref[...]
    prev = xp_ref[...]
    conv_cols = [slice(c0, c0 + col_chunk) for c0 in range(0, DN_QKV, col_chunk)]
    seg_cols = [(slice(c0, min(c0 + col_chunk, off + width)), epilogue)
                for off, width, epilogue in _AB_SEGMENTS for c0 in range(off, off + width, col_chunk)]

    def conv_chunk(cs, raw_full, half):
        cs = slice(cs.start + half * LANE, cs.start + (half + 1) * LANE)
        raw = raw_full[:, half * LANE:(half + 1) * LANE]
        xa = jnp.concatenate([prev[:, cs], raw], axis=0)
        y = pltpu.roll(xa, n - (SUBLANE - 3), axis=0)[0:tm] * cw[0:1, cs]
        y = y + pltpu.roll(xa, n - (SUBLANE - 2), axis=0)[0:tm] * cw[1:2, cs]
        y = y + pltpu.roll(xa, n - (SUBLANE - 1), axis=0)[0:tm] * cw[2:3, cs]
        y = y + raw * cw[3:4, cs]
        o_ref[:, cs] = _silu(y)
        last = raw[tm - SUBLANE:tm, :]
        xp_ref[:, cs] = last
        convn_ref[:, cs] = last

    def seg_chunk(cs, epilogue):
        r = _mm(xn, w_ref[:, cs])
        if epilogue == "silu":
            r = _silu(r)
        elif epilogue == "gla_q_scale":
            r = r * (GLA_DK ** -0.5)
        o_ref[:, cs] = r

    pending = []
    for j in range(max(len(conv_cols), len(seg_cols))):
        if j < len(conv_cols):
            raw = _mm(xn, w_ref[:, conv_cols[j]])
            if pending:
                conv_chunk(*pending.pop(0))
            pending += [(conv_cols[j], raw, half) for half in range(col_chunk // LANE)]
        if j < len(seg_cols):
            seg_chunk(*seg_cols[j])
            if pending:
                conv_chunk(*pending.pop(0))
    for item in pending:
        conv_chunk(*item)


def _pre_ab(x, g, w, conv0, convw, tm, rows_per_stream):
    t, d = x.shape
    nb = rows_per_stream // tm
    assert nb * tm == rows_per_stream, (rows_per_stream, tm)
    nstreams = t // rows_per_stream
    state_spec = pl.BlockSpec((None, SUBLANE, DN_QKV), lambda i: (i // nb, 0, 0))
    return pl.pallas_call(
        functools.partial(_pre_ab_kernel, nb=nb, col_chunk=256),
        grid=(t // tm,),
        in_specs=[pl.BlockSpec((tm, d), lambda i: (i, 0)),
                  _const_spec((1, d)),
                  _const_spec((d, AB_PAD)),
                  state_spec,
                  _const_spec(convw.shape)],
        out_specs=[pl.BlockSpec((tm, AB_PAD), lambda i: (i, 0)), state_spec],
        out_shape=[jax.ShapeDtypeStruct((t, AB_PAD), F32),
                   jax.ShapeDtypeStruct((nstreams, SUBLANE, DN_QKV), F32)],
        scratch_shapes=[pltpu.VMEM((SUBLANE, DN_QKV), F32)],
        compiler_params=pltpu.CompilerParams(dimension_semantics=("arbitrary",),
                                             vmem_limit_bytes=VMEM_LIMIT),
        name="pre_ab",
    )(x, g.reshape(1, d), w, conv0, convw)


def _pre_qkv_kernel(x_ref, g_ref, w_ref, o_ref, k_ref, v_ref, *, col_chunk):
    xn = _rms(x_ref[...], g_ref[...]).astype(BF16)
    dm = k_ref.shape[-1]
    for c0 in range(0, 3 * dm, col_chunk):
        c1 = c0 + col_chunk
        r = _mm(xn, w_ref[:, c0:c1])
        if c1 <= dm:
            o_ref[:, c0:c1] = (r * ATT_QSCALE).astype(BF16)
        else:
            o_ref[:, c0:c1] = r.astype(BF16)
            kv_ref = k_ref if c1 <= 2 * dm else v_ref
            kv_ref[:, c0 % dm:c0 % dm + col_chunk] = r


def _pre_qkv(x, g, w, tm, rows_per_stream):
    t, d = x.shape
    dm = w.shape[1] // 3
    nb = max(rows_per_stream // tm, 1)
    tail = pl.BlockSpec((tm, dm), lambda i: (i // nb, 0))
    return pl.pallas_call(
        functools.partial(_pre_qkv_kernel, col_chunk=512),
        grid=(t // tm,),
        in_specs=[pl.BlockSpec((tm, d), lambda i: (i, 0)),
                  _const_spec((1, d)),
                  _const_spec((d, 3 * dm))],
        out_specs=[pl.BlockSpec((tm, 3 * dm), lambda i: (i, 0)), tail, tail],
        out_shape=[jax.ShapeDtypeStruct((t, 3 * dm), BF16),
                   jax.ShapeDtypeStruct((t // nb, dm), F32),
                   jax.ShapeDtypeStruct((t // nb, dm), F32)],
        compiler_params=pltpu.CompilerParams(dimension_semantics=("arbitrary",),
                                             vmem_limit_bytes=VMEM_LIMIT),
        name="pre_qkv",
    )(x, g.reshape(1, d), w)


def _post_kernel(o_ref, x_ref, wo_ref, g_ref, wgu_ref, wd_ref, gf_ref, out_ref, acc_ref, *, ff_chunk, final):
    x1 = x_ref[...] + _mm(o_ref[...], wo_ref[...])
    xn = _rms(x1, g_ref[...]).astype(BF16)
    dff = wd_ref.shape[0]
    for f0 in range(0, dff, ff_chunk):
        gate = _mm(xn, wgu_ref[:, f0:f0 + ff_chunk])
        up = _mm(xn, wgu_ref[:, dff + f0:dff + f0 + ff_chunk])
        a = (gate * jax.nn.sigmoid(gate) * up).astype(BF16)
        part = _mm(a, wd_ref[f0:f0 + ff_chunk, :])
        if f0 == 0:
            acc_ref[...] = part
        else:
            acc_ref[...] += part
    x2 = x1 + acc_ref[...]
    if final:
        x2 = _rms(x2, gf_ref[...])
    out_ref[...] = x2


def _post(o, x, wo, g, wgu, wd, gf, tm, final):
    t, d = x.shape
    dm = o.shape[1]
    dff = wd.shape[0]
    return pl.pallas_call(
        functools.partial(_post_kernel, ff_chunk=256, final=final),
        grid=(t // tm,),
        in_specs=[pl.BlockSpec((tm, dm), lambda i: (i, 0)),
                  pl.BlockSpec((tm, d), lambda i: (i, 0)),
                  _const_spec((dm, d)),
                  _const_spec((1, d)),
                  _const_spec((d, 2 * dff)),
                  _const_spec((dff, d)),
                  _const_spec((1, d))],
        out_specs=pl.BlockSpec((tm, d), lambda i: (i, 0)),
        out_shape=jax.ShapeDtypeStruct((t, d), F32),
        scratch_shapes=[pltpu.VMEM((tm, d), F32)],
        compiler_params=pltpu.CompilerParams(dimension_semantics=("parallel",),
                                             vmem_limit_bytes=VMEM_LIMIT),
        name="post_ffn",
    )(o, x, wo, g.reshape(1, d), wgu, wd, gf.reshape(1, d))


def _score_group_of_fg(fg, c):
    return (2 * fg * c) // LANE


@functools.lru_cache(maxsize=None)
def _gla_pairs(c):
    hs = LANE // c
    pairs = []
    for sg in range(H * c // LANE):
        for qd in range(GW // LANE):
            if set(range(sg * hs, (sg + 1) * hs)) & set(range(4 * qd, 4 * qd + 4)):
                pairs.append((sg, qd))
    return tuple(pairs)


@functools.lru_cache(maxsize=None)
def _ab_consts(c):
    w = H * c
    hs = LANE // c
    nsb = c // GLA_SUB
    i = np.arange(c)
    r = np.arange(LANE)
    w_pos = np.arange(w) % c
    ltri = (i[None, :] <= i[:, None])
    eye_w = (i[:, None] == w_pos[None, :])
    causal_w = (w_pos[None, :] <= i[:, None])
    strict_w = (w_pos[None, :] < i[:, None])
    dmask = causal_w & ((i[:, None] // GLA_SUB) == (w_pos[None, :] // GLA_SUB))
    row_hl = r // c
    row_j = r % c
    m_ss = (row_hl[:, None] == row_hl[None, :])
    m_ff = ((r // DN_DV)[:, None] == (r // DN_DV)[None, :])
    m_sf = np.stack([(_score_group_of_fg(fg, c) * hs + row_hl)[:, None] == (2 * fg + r // DN_DV)[None, :]
                     for fg in range(NFG)])
    pairs = _gla_pairs(c)
    m_sq = np.stack([(sg * hs + row_hl)[:, None] == (4 * qd + r // GLA_DK)[None, :] for sg, qd in pairs])
    m_sq3 = np.stack([np.concatenate([m & ((row_j // GLA_SUB) == j)[:, None] for j in range(max(nsb - 1, 1))],
                                     axis=1) for m in m_sq])
    m_fq = np.stack([(2 * fg + r // DN_DV)[:, None] == (4 * (fg // 2) + r // GLA_DK)[None, :]
                     for fg in range(NFG)])
    expg = np.zeros((LANE, HW), np.float32)
    exps = np.zeros((LANE, w), np.float32)
    expb = np.zeros((LANE, HW), np.float32)
    for h in range(H):
        expg[h, DN_DV * h:DN_DV * (h + 1)] = 1.0
        exps[h, c * h:c * (h + 1)] = 1.0
        expb[H + h, DN_DV * h:DN_DV * (h + 1)] = 1.0
    f = lambda a: jnp.asarray(a, F32)
    b = lambda a: jnp.asarray(a, BF16)
    return dict(ltri=b(ltri), eye_w=f(eye_w), causal_w=f(causal_w), strict_w=f(strict_w), dmask=f(dmask),
                m_ss=b(m_ss), m_sf=b(m_sf), m_ffb=b(m_ff), m_ff=f(m_ff), m_sq=b(m_sq), m_sq3=b(m_sq3),
                m_fq=f(m_fq), expg=b(expg), exps=b(exps), expb=b(expb))


_AB_CONST_ORDER = ("ltri", "eye_w", "causal_w", "strict_w", "dmask", "m_ss", "m_sf", "m_ffb", "m_ff",
                   "m_sq", "m_sq3", "m_fq", "expg", "exps", "expb")


def _headsum(xs, bones):
    rows = xs[0].shape[0]
    pieces = [_grp(p, g) for x in xs for p in _split(x, 2) for g in range(NFG)]
    half = len(pieces) // 2
    y = jnp.concatenate(_group_mm([jnp.concatenate(pieces[:half], axis=0), jnp.concatenate(pieces[half:], axis=0)],
                                 [bones, bones]), axis=0)
    outs = []
    for n in range(len(xs)):
        base = 2 * NFG * n
        outs.append(jnp.concatenate(
            [y[(base + g) * rows:(base + g + 1) * rows] + y[(base + NFG + g) * rows:(base + NFG + g + 1) * rows]
             for g in range(NFG)], axis=1))
    return outs


def _ab_kernel(proj_ref, dn0_ref, gl0_ref, gparam_ref, w2_ref, gkb_ref, onorm_ref,
               ltri_ref, eye_ref, causal_ref, strict_ref, dmask_ref, mss_ref, msf_ref, mffb_ref, mff_ref,
               msq_ref, msq3_ref, mfq_ref, expg_ref, exps_ref, expb_ref,
               o_ref, dns_ref, gls_ref,
               sdn_ref, sgl_ref, *, c, nc, ns):
    i = pl.program_id(1)
    nsb = c // GLA_SUB
    hs = LANE // c
    nsg = H * c // LANE
    sg_of = [_score_group_of_fg(fg, c) for fg in range(NFG)]
    streams = range(ns)

    @pl.when(i == 0)
    def _load_state():
        sdn_ref[...] = dn0_ref[...]
        sgl_ref[...] = gl0_ref[...]

    bones = mffb_ref[...]
    mss = mss_ref[...]
    ltri = ltri_ref[...]
    gp = gparam_ref[...]
    onorm = onorm_ref[...]

    q = [proj_ref[s, :, OFF_QKV:OFF_QKV + HW] for s in streams]
    k = [proj_ref[s, :, OFF_QKV + HW:OFF_QKV + 2 * HW] for s in streams]
    v = [proj_ref[s, :, OFF_QKV + 2 * HW:OFF_QKV + 3 * HW] for s in streams]

    ssq = _headsum([q[s] * q[s] for s in streams] + [k[s] * k[s] for s in streams], bones)
    for s in streams:
        q[s] = q[s] * lax.rsqrt(ssq[s] + EPS) * (DN_DK ** -0.5)
        k[s] = k[s] * lax.rsqrt(ssq[ns + s] + EPS)

    pairs = _gla_pairs(c)
    small = [proj_ref[s, :, OFF_SMALL:OFF_SMALL + LANE] for s in streams]
    gl = {}

    def gla_gate_logits():
        z = _mm(jnp.concatenate([small[s].astype(BF16) for s in streams], axis=0), w2_ref[...]) + gkb_ref[...]
        gl["z"] = [z[c * s:c * (s + 1)] for s in streams]

    def gla_cumsum():
        glog = jnp.concatenate([-_softplus(-gl["z"][s]) * (1.0 / GLA_NORMALIZER) for s in streams], axis=1)
        bcum = _mmx_left(ltri, glog, 3)
        gl["bcum"] = [bcum[:, GW * s:GW * (s + 1)] for s in streams]

    def gla_scores_and_state():
        gl["a"], gl["o_inter"], gl["v2b"] = [], [], []
        for s in streams:
            bcum = gl["bcum"][s]
            q2 = proj_ref[s, :, OFF_GQ:OFF_GQ + GW]
            k2 = proj_ref[s, :, OFF_GK:OFF_GK + GW]
            v2b = proj_ref[s, :, OFF_GV:OFF_GV + HW].astype(BF16)
            blast = bcum[c - 1:c, :]
            qe = (q2 * jnp.exp(bcum)).astype(BF16)
            kdec = (k2 * jnp.exp(blast - bcum)).astype(BF16)
            eblast = jnp.exp(blast)
            rmid = jnp.concatenate(
                [jnp.broadcast_to(bcum[GLA_SUB * t + GLA_SUB // 2:GLA_SUB * t + GLA_SUB // 2 + 1, :], (GLA_SUB, GW))
                 for t in range(nsb)], axis=0)
            qm = (q2 * jnp.exp(bcum - rmid)).astype(BF16)
            km = (k2 * jnp.exp(rmid - bcum)).astype(BF16)
            parts = [None] * nsg
            prods = _group_mm([_grp(qm, qd) for _, qd in pairs],
                             [_tile_rows(_grp(km, qd), hs) * msq_ref[n] for n, (_, qd) in enumerate(pairs)], nt=True)
            for n, (sg, _) in enumerate(pairs):
                parts[sg] = prods[n] if parts[sg] is None else parts[sg] + prods[n]
            a_gl = dmask_ref[...] * jnp.concatenate(parts, axis=1)
            if nsb > 1:
                rend = jnp.concatenate(
                    [jnp.broadcast_to(bcum[GLA_SUB * (t + 1) - 1:GLA_SUB * (t + 1), :], (GLA_SUB, GW))
                     for t in range(nsb)], axis=0)
                kr = (k2 * jnp.exp(rend - bcum)).astype(BF16)
                rowi = lax.broadcasted_iota(jnp.int32, (c, GW), 0)
                qs = []
                for t in range(nsb - 1):
                    e = jnp.where(rowi >= GLA_SUB * (t + 1), bcum - rend[GLA_SUB * t:GLA_SUB * t + 1, :], -jnp.inf)
                    qs.append((q2 * jnp.exp(e)).astype(BF16))
                parts = [None] * nsg
                prods = _group_mm(
                    [jnp.concatenate([_grp(x, qd) for x in qs], axis=1) for _, qd in pairs],
                    [_tile_rows(jnp.concatenate([_grp(kr, qd)] * (nsb - 1), axis=1), hs) * msq3_ref[n]
                     for n, (_, qd) in enumerate(pairs)], nt=True)
                for n, (sg, _) in enumerate(pairs):
                    parts[sg] = prods[n] if parts[sg] is None else parts[sg] + prods[n]
                a_gl = a_gl + jnp.concatenate(parts, axis=1)
            st = [sgl_ref[s, fg] for fg in range(NFG)]
            o_inter = _group_mm([_grp(qe, fg // 2) for fg in range(NFG)], [x.astype(BF16) for x in st], nt=True)
            for fg in range(NFG):
                qd = fg // 2
                sgl_ref[s, fg] = st[fg] * _grp(eblast, qd) + _mm_tn(_grp(v2b, fg), _grp(kdec, qd)) * mfq_ref[fg]
            gl["a"].append(a_gl.astype(BF16))
            gl["o_inter"].append(o_inter)
            gl["v2b"].append(v2b)

    def gla_out():
        gl["o"] = []
        for s in streams:
            intra = _group_mm([_grp(gl["a"][s], sg_of[fg]) for fg in range(NFG)],
                             [_tile_rows(_grp(gl["v2b"][s], fg), hs) * msf_ref[fg] for fg in range(NFG)])
            gl["o"].append(jnp.concatenate([gl["o_inter"][s][fg] + intra[fg] for fg in range(NFG)], axis=1))

    def rows_of(x, s):
        return x[c * s:c * (s + 1)]

    gla_gate_logits()
    g_all = jnp.concatenate([-jnp.exp(gp[0:1, :]) * _softplus(small[s] + gp[1:2, :]) for s in streams], axis=1)
    gcum_all = _mmx_left(ltri, g_all, 3)
    gla_cumsum()
    gcum_parts = _split(jnp.concatenate([_grp(gcum_all, s) for s in streams], axis=0), 3)
    gw_all = _mm(gcum_parts[0], expg_ref[...]) + _mm(gcum_parts[1], expg_ref[...]) + _mm(gcum_parts[2], expg_ref[...])
    if c == DN_DV:
        gs_all = gw_all
    else:
        gs_all = (_mm(gcum_parts[0], exps_ref[...]) + _mm(gcum_parts[1], exps_ref[...])
                  + _mm(gcum_parts[2], exps_ref[...]))
    bw_all = _mmx(jnp.concatenate([jax.nn.sigmoid(small[s]) for s in streams], axis=0), expb_ref[...], 2)
    gcum_w = [rows_of(gw_all, s) for s in streams]
    beta_w = [rows_of(bw_all, s) for s in streams]
    eg_w = [jnp.exp(gcum_w[s]) for s in streams]
    edec_w = [jnp.exp(gcum_w[s][c - 1:c, :] - gcum_w[s]) for s in streams]
    eglast_w = [jnp.exp(gcum_w[s][c - 1:c, :]) for s in streams]
    gla_scores_and_state()

    decay_w = []
    for s in streams:
        gcum_s = rows_of(gs_all, s)
        d_w = gcum_s - jnp.sum(gcum_s * eye_ref[...], axis=0, keepdims=True)
        decay_w.append(jnp.exp(jnp.where(causal_ref[...] > 0.0, d_w, -jnp.inf)))

    kb = [k[s] * beta_w[s] for s in streams]
    a_w, qkd_b = [], []
    for s in streams:
        kq = jnp.concatenate([kb[s], q[s]], axis=0).astype(BF16)
        kbf = k[s].astype(BF16)
        parts = [None] * nsg
        prods = _group_mm([_grp(kq, fg) for fg in range(NFG)],
                         [_tile_rows(_grp(kbf, fg), hs) * msf_ref[fg] for fg in range(NFG)], nt=True)
        for fg in range(NFG):
            parts[sg_of[fg]] = prods[fg] if parts[sg_of[fg]] is None else parts[sg_of[fg]] + prods[fg]
        r = jnp.concatenate(parts, axis=1)
        a_w.append(strict_ref[...] * r[0:c] * decay_w[s])
        qkd_b.append((r[c:2 * c] * decay_w[s]).astype(BF16))

    def ssprod(x_w, y_w):
        xb = x_w.astype(BF16)
        yb = y_w.astype(BF16)
        return jnp.concatenate(_group_mm([_grp(xb, sg) for sg in range(nsg)],
                                        [_tile_rows(_grp(yb, sg), hs) * mss for sg in range(nsg)]), axis=1)

    nlev = int(math.log2(c))
    pw = [-a_w[s] for s in streams]
    t_w = [eye_ref[...] + pw[s] for s in streams]
    pw = [ssprod(pw[s], pw[s]) for s in streams]
    gla_out()
    for _ in range(2, nlev):
        rr = [ssprod(jnp.concatenate([pw[s], t_w[s]], axis=0), pw[s]) for s in streams]
        t_w = [t_w[s] + rr[s][c:2 * c] for s in streams]
        pw = [rr[s][0:c] for s in streams]
    rr = [ssprod(t_w[s], pw[s]) for s in streams]
    tb = [(t_w[s] + rr[s]).astype(BF16) for s in streams]

    vbb = [(v[s] * beta_w[s]).astype(BF16) for s in streams]
    kbe = [(kb[s] * eg_w[s]).astype(BF16) for s in streams]
    qeg = [q[s] * eg_w[s] for s in streams]
    kdc = [(k[s] * edec_w[s]).astype(BF16) for s in streams]
    fgs = range(NFG)
    uw, rs, s_dn, vnb = {}, {}, {}, {}
    for s in streams:
        uw[s] = _group_mm([_grp(tb[s], sg_of[fg]) for fg in fgs],
                         [jnp.concatenate([_tile_rows(_grp(vbb[s], fg), hs) * msf_ref[fg],
                                           _tile_rows(_grp(kbe[s], fg), hs) * msf_ref[fg]], axis=1) for fg in fgs])
    for s in streams:
        s_dn[s] = [sdn_ref[s, fg] for fg in fgs]
        rs[s] = _group_mm([jnp.concatenate([uw[s][fg][:, LANE:2 * LANE], _grp(qeg[s], fg)], axis=0).astype(BF16)
                          for fg in fgs], [s_dn[s][fg].astype(BF16) for fg in fgs])
    o_dn = []
    for s in streams:
        vnb[s] = [(uw[s][fg][:, 0:LANE] - rs[s][fg][0:c]).astype(BF16) for fg in fgs]
        intra = _group_mm([_grp(qkd_b[s], sg_of[fg]) for fg in fgs],
                         [_tile_rows(vnb[s][fg], hs) * msf_ref[fg] for fg in fgs])
        o_dn.append(jnp.concatenate([rs[s][fg][c:2 * c] + intra[fg] for fg in fgs], axis=1))
    for s in streams:
        for fg in fgs:
            sdn_ref[s, fg] = (s_dn[s][fg] * _grp(eglast_w[s], fg)
                              + _mm_tn(_grp(kdc[s], fg), vnb[s][fg]) * mff_ref[...])
    o_gl = gl["o"]

    oss = _headsum([o_dn[s] * o_dn[s] for s in streams] + [o_gl[s] * o_gl[s] for s in streams], bones)
    for s in streams:
        md, mgl = oss[s], oss[ns + s]
        gd = proj_ref[s, :, OFF_DGATE:OFF_DGATE + HW]
        gg = proj_ref[s, :, OFF_GGATE:OFF_GGATE + HW]
        y_dn = (o_dn[s] * lax.rsqrt(md * (1.0 / DN_DV) + EPS)) * onorm[0:1, :] * gd
        y_gl = (o_gl[s] * lax.rsqrt(mgl * (1.0 / GLA_DV) + EPS)) * onorm[1:2, :] * gg
        o_ref[s, :, 0:HW] = y_dn.astype(BF16)
        o_ref[s, :, HW:2 * HW] = y_gl.astype(BF16)

    @pl.when(i == nc - 1)
    def _store_state():
        dns_ref[...] = sdn_ref[...]
        gls_ref[...] = sgl_ref[...]


def _ab_mixer(proj, dn0, gl0, gparam, w2p, gkb, onorm, c):
    b, l, _ = proj.shape
    nc = l // c
    ns = AB_STREAMS
    consts = _ab_consts(c)
    cvals = [consts[n] for n in _AB_CONST_ORDER]
    per_b = lambda shape: pl.BlockSpec((ns,) + shape, lambda bi, ci: (bi,) + (0,) * len(shape))
    state = (NFG, LANE, LANE)
    in_specs = ([pl.BlockSpec((ns, c, AB_PAD), lambda bi, ci: (bi, ci, 0)),
                 per_b(state), per_b(state),
                 _const_spec(gparam.shape), _const_spec(w2p.shape),
                 _const_spec(gkb.shape), _const_spec(onorm.shape)]
                + [_const_spec(a.shape) for a in cvals])
    out_specs = [pl.BlockSpec((ns, c, 2 * HW), lambda bi, ci: (bi, ci, 0)),
                 per_b(state), per_b(state)]
    out_shape = [jax.ShapeDtypeStruct((b, l, 2 * HW), BF16),
                 jax.ShapeDtypeStruct((b,) + state, F32),
                 jax.ShapeDtypeStruct((b,) + state, F32)]
    return pl.pallas_call(
        functools.partial(_ab_kernel, c=c, nc=nc, ns=ns),
        grid=(b // ns, nc),
        in_specs=in_specs,
        out_specs=out_specs,
        out_shape=out_shape,
        scratch_shapes=[pltpu.VMEM((ns,) + state, F32),
                        pltpu.VMEM((ns,) + state, F32)],
        compiler_params=pltpu.CompilerParams(dimension_semantics=("parallel", "arbitrary"),
                                             vmem_limit_bytes=VMEM_LIMIT),
        name="ab_mixer",
    )(proj, dn0, gl0, gparam, w2p, gkb, onorm, *cvals)


_DN_PLACE = np.eye(2, dtype=np.float32)
_GLA_PLACE = np.zeros((2, 2, 4), np.float32)
for _fp in range(2):
    for _hl in range(2):
        _GLA_PLACE[_fp, _hl, 2 * _fp + _hl] = 1.0


def _dn_state_to_groups(s):
    b = s.shape[0]
    s5 = s.reshape(b, NFG, 2, DN_DK, 1, DN_DV)
    return (s5 * _DN_PLACE[None, None, :, None, :, None]).reshape(b, NFG, LANE, LANE)


def _dn_state_from_groups(g):
    b = g.shape[0]
    g6 = g.reshape(b, NFG, 2, DN_DK, 2, DN_DV)
    return jnp.sum(g6 * _DN_PLACE[None, None, :, None, :, None], axis=4).reshape(b, DN_HEADS, DN_DK, DN_DV)


def _gla_state_to_groups(s):
    b = s.shape[0]
    st = jnp.swapaxes(s, -1, -2).reshape(b, 2, 2, 2, GLA_DV, 1, GLA_DK)
    return (st * _GLA_PLACE[None, None, :, :, None, :, None]).reshape(b, NFG, LANE, LANE)


def _gla_state_from_groups(g):
    b = g.shape[0]
    g7 = g.reshape(b, 2, 2, 2, GLA_DV, 4, GLA_DK)
    st = jnp.sum(g7 * _GLA_PLACE[None, None, :, :, None, :, None], axis=5)
    return jnp.swapaxes(st.reshape(b, GLA_HEADS, GLA_DV, GLA_DK), -1, -2)


def _softmax2_rows(parts):
    m = None
    for s in parts:
        t = jnp.max(s, axis=-1, keepdims=True)
        m = t if m is None else jnp.maximum(m, t)
    es = [jnp.exp2(s - m) for s in parts]
    l = None
    for e in es:
        t = jnp.sum(e, axis=-1, keepdims=True)
        l = t if l is None else l + t
    inv = 1.0 / l
    return [e * inv for e in es]


def _fold_lanes(x, op):
    out = x[:, 0:LANE]
    for c0 in range(LANE, x.shape[1], LANE):
        out = op(out, x[:, c0:c0 + LANE])
    return out


def _attn_prompt_kernel(q_ref, k0_ref, k1_ref, k2_ref, v0_ref, v1_ref, v2_ref, bias_ref, o_ref, *, qb):
    i = pl.program_id(1)
    offs = (jnp.where(i >= 2, 0.0, -jnp.inf), jnp.where(i >= 1, 0.0, -jnp.inf), 0.0)
    k_refs = (k0_ref, k1_ref, k2_ref)
    v_refs = (v0_ref, v1_ref, v2_ref)

    nq = qb // CHUNK
    tpb = qb // LANE
    tile_of, _, _, row_const = _bias_tiles(qb)

    def bias_tile(h, qc, lt):
        u = tile_of[qc, lt]
        return bias_ref[h, u, 0:1, :] if row_const[u] else bias_ref[h, u]

    def scores(h):
        sl = slice(C_HD * h, C_HD * (h + 1))
        qh = q_ref[:, sl]
        return [_mm_nt(qh, k_refs[t][:, sl]) for t in range(3)]

    def finish(h, ss):
        sl = slice(C_HD * h, C_HD * (h + 1))
        s_t, ems = {}, []
        for qc in range(nq):
            rows = slice(CHUNK * qc, CHUNK * (qc + 1))
            em = None
            for t in range(3):
                et = None
                for j in range(tpb):
                    lt = t * tpb + j
                    if (qc, lt) in tile_of:
                        s = ss[t][rows, LANE * j:LANE * (j + 1)] + bias_tile(h, qc, lt)
                        s_t[qc, lt] = s
                        et = s if et is None else jnp.maximum(et, s)
                if et is not None:
                    et = et + offs[t]
                    em = et if em is None else jnp.maximum(em, et)
            ems.append(em)
        m = jnp.max(jnp.concatenate(ems, axis=0), axis=-1, keepdims=True)
        shift = [m - offs[t] for t in range(3)]
        p_tiles, els = [], []
        for qc in range(nq):
            rows = slice(CHUNK * qc, CHUNK * (qc + 1))
            el, row_tiles = None, []
            for lt in range(3 * tpb):
                if (qc, lt) in s_t:
                    p = jnp.exp2(s_t[qc, lt] - shift[lt // tpb][rows])
                    el = p if el is None else el + p
                    row_tiles.append(p.astype(BF16))
                else:
                    row_tiles.append(jnp.zeros((CHUNK, LANE), BF16))
            els.append(el)
            p_tiles.append(row_tiles)
        inv_l = [1.0 / jnp.sum(jnp.concatenate(els, axis=0), axis=-1, keepdims=True)]
        acc = None
        for t in range(3):
            p_blk = jnp.concatenate(
                [jnp.concatenate(p_tiles[qc][t * tpb:(t + 1) * tpb], axis=1) for qc in range(nq)], axis=0)
            at = _mm(p_blk, v_refs[t][:, sl])
            acc = at if acc is None else acc + at
        o_ref[:, sl] = (acc * jnp.concatenate(inv_l, axis=0)).astype(BF16)

    ss_next = scores(0)
    for h in range(C_HEADS):
        ss = ss_next
        if h + 1 < C_HEADS:
            ss_next = scores(h + 1)
        finish(h, ss)


def _attn_prompt(qkv, bias, qb):
    b, l, _ = qkv.shape
    dm = C_HEADS * C_HD
    kv_spec = lambda colblk, back: pl.BlockSpec(
        (None, qb, dm), lambda bi, i: (bi, jnp.maximum(i - back, 0), colblk))
    return pl.pallas_call(
        functools.partial(_attn_prompt_kernel, qb=qb),
        grid=(b, l // qb),
        in_specs=[pl.BlockSpec((None, qb, dm), lambda bi, i: (bi, i, 0)),
                  kv_spec(1, 2), kv_spec(1, 1), kv_spec(1, 0),
                  kv_spec(2, 2), kv_spec(2, 1), kv_spec(2, 0),
                  _const_spec(bias.shape)],
        out_specs=pl.BlockSpec((None, qb, dm), lambda bi, i: (bi, i, 0)),
        out_shape=jax.ShapeDtypeStruct((b, l, dm), BF16),
        compiler_params=pltpu.CompilerParams(dimension_semantics=("parallel", "parallel"),
                                             vmem_limit_bytes=VMEM_LIMIT),
        name="attn_prompt",
    )(qkv, qkv, qkv, qkv, qkv, qkv, qkv, bias)


def _attn_sample_kernel(q_ref, k_ref, v_ref, ck_ref, cv_ref, bc_ref, bn_ref, o_ref):
    sls = [slice(C_HD * h, C_HD * (h + 1)) for h in range(C_HEADS)]
    scores = [(_mm_nt(q_ref[:, sl], ck_ref[:, sl].astype(BF16)) + bc_ref[h],
               _mm_nt(q_ref[:, sl], k_ref[:, sl]) + bn_ref[h]) for h, sl in enumerate(sls)]
    probs = [_softmax2_rows(list(s)) for s in scores]
    for (pc, pn), sl in zip(probs, sls):
        oh = _mm(pc.astype(BF16), cv_ref[:, sl].astype(BF16)) + _mm(pn.astype(BF16), v_ref[:, sl])
        o_ref[:, sl] = oh.astype(BF16)


def _attn_sample(qkv, ck, cv, bias_c, bias_n):
    b, l, _ = qkv.shape
    dm = C_HEADS * C_HD
    ncache = ck.shape[1]
    new_spec = lambda colblk: pl.BlockSpec((None, l, dm), lambda bi: (bi, 0, colblk))
    cache_spec = pl.BlockSpec((None, ncache, dm), lambda bi: (bi, 0, 0))
    return pl.pallas_call(
        _attn_sample_kernel,
        grid=(b,),
        in_specs=[new_spec(0), new_spec(1), new_spec(2), cache_spec, cache_spec,
                  _const_spec(bias_c.shape), _const_spec(bias_n.shape)],
        out_specs=pl.BlockSpec((None, l, dm), lambda bi: (bi, 0, 0)),
        out_shape=jax.ShapeDtypeStruct((b, l, dm), BF16),
        compiler_params=pltpu.CompilerParams(dimension_semantics=("parallel",),
                                             vmem_limit_bytes=VMEM_LIMIT),
        name="attn_sample",
    )(qkv, qkv, qkv, ck, cv, bias_c, bias_n)


@functools.lru_cache(maxsize=None)
def _bias_tiles(qb):
    nq, ntile, cpt = qb // CHUNK, 3 * qb // LANE, LANE // CHUNK
    r = np.arange(CHUNK)[:, None]
    col = np.arange(LANE)[None, :]
    n = np.arange(CHUNK + LANE - 1)
    ids, tile_of, gens, masks, row_const = {}, {}, [], [], []
    for qc in range(nq):
        lo, hi = qc + 2 * nq - C_BAND_CHUNKS, qc + 2 * nq
        for lt in range(ntile):
            if cpt * lt + cpt - 1 < lo or cpt * lt > hi:
                continue
            base = CHUNK * qc + 2 * qb - LANE * lt
            rel = np.clip(base + r - col, -MAX_REL, MAX_REL) + MAX_REL
            kc = (LANE * lt + col) // CHUNK
            band = np.broadcast_to((kc >= lo) & (kc <= hi), (CHUNK, LANE))
            key = (rel.tobytes(), band.tobytes())
            if key not in ids:
                ids[key] = len(gens)
                gens.append(np.clip(base + CHUNK - 1 - n, -MAX_REL, MAX_REL) + MAX_REL)
                masks.append(band)
                row_const.append(bool((rel == rel[0:1]).all()))
            tile_of[qc, lt] = ids[key]
    return tile_of, np.stack(gens), np.stack(masks), tuple(row_const)


def _prompt_bias(table, qb):
    _, gens, masks, _ = _bias_tiles(qb)
    nh, nu = table.shape[0], gens.shape[0]
    period = CHUNK + LANE
    w = jnp.concatenate([table[:, gens], jnp.zeros((nh, nu, 1), table.dtype)], axis=2)
    w = jnp.roll(w, -(CHUNK - 1), axis=2)
    tiles = jnp.tile(w, (1, 1, CHUNK))[:, :, :CHUNK * (period - 1)].reshape(nh, nu, CHUNK, period - 1)[..., :LANE]
    return jnp.where(jnp.asarray(masks)[None], tiles.astype(F32) * LOG2E, -jnp.inf)


def _sample_bias(table, l, ncache):
    qpos = PAST_LEN + np.arange(l)
    kpos = np.concatenate([PAST_LEN - ncache + np.arange(ncache), PAST_LEN + np.arange(l)])
    rel = np.clip(qpos[:, None] - kpos[None, :], -MAX_REL, MAX_REL) + MAX_REL
    qch = qpos // CHUNK
    kch = kpos // CHUNK
    valid = (kch[None, :] <= qch[:, None]) & (kch[None, :] >= qch[:, None] - C_BAND_CHUNKS)
    bias = jnp.where(jnp.asarray(valid)[None], table[:, rel].astype(F32) * LOG2E, -jnp.inf)
    return bias[:, :, :ncache], bias[:, :, ncache:]


def _permute_w_in(w):
    o_a = DN_QKV
    o_b = o_a + DN_HEADS
    o_gate = o_b + DN_HEADS
    o_gq = o_gate + HW
    o_gk = o_gq + GW
    o_gv = o_gk + GW
    o_lr = o_gv + HW
    o_gg = o_lr + GLA_RANK
    return jnp.concatenate(
        [w[..., :DN_QKV], w[..., o_gate:o_gq], w[..., o_gq:o_gk], w[..., o_gk:o_gv], w[..., o_gv:o_lr],
         w[..., o_gg:o_gg + HW], w[..., o_a:o_gate], w[..., o_lr:o_gg],
         jnp.zeros(w.shape[:-1] + (LANE - 2 * DN_HEADS - GLA_RANK,), w.dtype)], axis=-1)


def _trunk(x, conv0, dn0, gla0, ck, cv, w, prompt, tm):
    b, l, d = x.shape
    depth = w["ffn_norm"].shape[0]
    c = min(CHUNK, l)
    xf = x.reshape(b * l, d)
    convs, dns, glas, ks, vs = [], [], [], [], []
    for layer in range(depth):
        i = layer // 2
        if layer % 2 == 0:
            if conv0 is None:
                conv0p = jnp.zeros((b, SUBLANE, DN_QKV), F32)
                dn0g = gl0g = jnp.zeros((b, NFG, LANE, LANE), F32)
            else:
                conv0p = jnp.pad(conv0[i], ((0, 0), (SUBLANE - (CONV_W - 1), 0), (0, 0)))
                dn0g, gl0g = _dn_state_to_groups(dn0[i]), _gla_state_to_groups(gla0[i])
            proj, convn = _pre_ab(xf, w["ab_norm"][i], w["ab_w_in"][i], conv0p, w["dn_conv_w"][i], min(tm, l), l)
            o, dn_s, gl_s = _ab_mixer(proj.reshape(b, l, AB_PAD), dn0g, gl0g, w["gparam"][i],
                                      w["gk_w2p"][i], w["gla_gk_b"][i], w["onorm"][i], c)
            convs.append(convn[:, SUBLANE - (CONV_W - 1):, :])
            dns.append(_dn_state_from_groups(dn_s))
            glas.append(_gla_state_from_groups(gl_s))
            wo = w["ab_w_out"][i]
        else:
            dm = C_HEADS * C_HD
            rows = min(C_PAST, l) if prompt else l
            assert l <= tm or rows == tm, (l, tm, rows)
            qkv, k_rows, v_rows = _pre_qkv(xf, w["c_norm"][i], w["c_w_qkv"][i], tm, l)
            qkv = qkv.reshape(b, l, 3 * dm)
            ks.append(k_rows.reshape(b, -1, C_HEADS, C_HD)[:, -rows:])
            vs.append(v_rows.reshape(b, -1, C_HEADS, C_HD)[:, -rows:])
            if prompt:
                o = _attn_prompt(qkv, _prompt_bias(w["c_rel_bias"][i], ATT_QB), ATT_QB)
            else:
                ncache = ck.shape[2]
                bias_c, bias_n = _sample_bias(w["c_rel_bias"][i], l, ncache)
                o = _attn_sample(qkv, ck[i].reshape(b, ncache, dm), cv[i].reshape(b, ncache, dm), bias_c, bias_n)
            wo = w["c_w_out"][i]
        xf = _post(o.reshape(b * l, -1), xf, wo, w["ffn_norm"][layer], w["ffn_w_gu"][layer],
                   w["ffn_w_down"][layer], w["final_norm"], tm, final=(layer == depth - 1))
    return (xf.reshape(b, l, d), jnp.stack(convs), jnp.stack(dns), jnp.stack(glas), jnp.stack(ks), jnp.stack(vs))


def kernel(x_prompt, x_sample, state_dn_conv, state_dn, state_gla, cache_c_k, cache_c_v, ab_norm, ab_w_in, dn_conv_w, dn_a_log, dn_dt_bias, dn_out_norm, gla_gk_w2, gla_gk_b, gla_out_norm, ab_w_out, c_norm, c_w_qkv, c_rel_bias, c_w_out, ffn_norm, ffn_w_gu, ffn_w_down, final_norm):
    n_ab = ab_w_in.shape[0]
    gparam = jnp.zeros((n_ab, SUBLANE, LANE), F32)
    gparam = gparam.at[:, 0, :DN_HEADS].set(dn_a_log).at[:, 1, :DN_HEADS].set(dn_dt_bias)
    gk_w2p = jnp.zeros((n_ab, LANE, GW), F32).at[:, 2 * DN_HEADS:2 * DN_HEADS + GLA_RANK, :].set(gla_gk_w2)
    onorm = jnp.stack([jnp.tile(dn_out_norm, (1, DN_HEADS)), jnp.tile(gla_out_norm, (1, GLA_HEADS))], axis=1)
    w = {
        "ab_norm": ab_norm, "c_norm": c_norm, "ffn_norm": ffn_norm, "final_norm": final_norm,
        "ab_w_in": _permute_w_in(ab_w_in.astype(BF16)),
        "dn_conv_w": dn_conv_w, "gparam": gparam, "gk_w2p": gk_w2p.astype(BF16),
        "gla_gk_b": gla_gk_b[:, None, :], "onorm": onorm,
        "ab_w_out": ab_w_out.astype(BF16), "c_w_qkv": c_w_qkv.astype(BF16), "c_rel_bias": c_rel_bias,
        "c_w_out": c_w_out.astype(BF16), "ffn_w_gu": ffn_w_gu.astype(BF16), "ffn_w_down": ffn_w_down.astype(BF16),
    }
    y_p, conv_p, dn_p, gla_p, ck_p, cv_p = _trunk(x_prompt, None, None, None, None, None, w, True, 512)
    ts = x_sample.shape[0] * x_sample.shape[1]
    y_s, conv_s, dn_s, gla_s, ck_s, cv_s = _trunk(x_sample, state_dn_conv, state_dn, state_gla,
                                                  cache_c_k, cache_c_v, w, False, ts)
    return (y_p, y_s, conv_p, conv_s, dn_p, dn_s, gla_p, gla_s, ck_p, ck_s, cv_p, cv_s)
```
